```python
import math
import jax, jax.numpy as jnp
from jax import lax
import numpy as np

D_MODEL = 1024
BATCH = 8
SEQ = 2048
DEPTH = 4
DEC_BATCH = 128
DEC_SEQ = 4
PAST_LEN = 2048
PAGE_SIZE = 128

HEAD_DIM = 64
SB_HEADS = 8
DF_HEADS = 4
SB_W = SB_HEADS * HEAD_DIM
DF_W = DF_HEADS * 2 * HEAD_DIM
MIX_W = SB_W + DF_W
IN_W = 3 * SB_W + 3 * DF_W
D_FF = 2816
NUM_BUCKETS = 32
MAX_EXACT = NUM_BUCKETS // 2
MAX_DISTANCE = 128
QUERY_BLOCK = 128
RMS_EPS = 1e-6
NEG_INF = -1e30

kernel_name = "hymba_stickbreak_diffattn_macaron"


def _rms(x, g):
    xf = x.astype(jnp.float32)
    y = xf * lax.rsqrt(jnp.mean(xf * xf, axis=-1, keepdims=True) + RMS_EPS)
    return (y * g.astype(jnp.float32)).astype(x.dtype)


def _swiglu(x, w_gate, w_up, w_down):
    return (jax.nn.silu(x @ w_gate) * (x @ w_up)) @ w_down


def _rel_bucket(dist):
    n = jnp.maximum(dist, 0)
    nf = jnp.maximum(n, 1).astype(jnp.float32)
    large = MAX_EXACT + (jnp.log(nf / MAX_EXACT) / math.log(MAX_DISTANCE / MAX_EXACT)
                         * (NUM_BUCKETS - MAX_EXACT)).astype(jnp.int32)
    large = jnp.minimum(large, NUM_BUCKETS - 1)
    return jnp.where(n < MAX_EXACT, n, large)


def _query_blocks(q, q_pos):
    b, t = q.shape[0], q.shape[1]
    blk = QUERY_BLOCK if t % QUERY_BLOCK == 0 else t
    nb = t // blk
    qb = jnp.moveaxis(q.reshape(b, nb, blk, *q.shape[2:]), 1, 0)
    return qb, q_pos.reshape(nb, blk)


def _unblock(ob):
    ob = jnp.moveaxis(ob, 0, 1)
    return ob.reshape(ob.shape[0], ob.shape[1] * ob.shape[2], *ob.shape[3:])


def _stick_breaking(q, k, v, q_pos, k_pos):
    scale = HEAD_DIM ** -0.5
    qb, pb = _query_blocks(q, q_pos)

    def one(args):
        qi, pi = args
        z = jnp.einsum('bqhd,bkhd->bhqk', qi, k).astype(jnp.float32) * scale
        mask = k_pos[None, :] < pi[:, None]
        log_stay = jnp.where(mask, jax.nn.log_sigmoid(-z), 0.0)
        after = lax.cumsum(log_stay, axis=3, reverse=True) - log_stay
        w = jnp.where(mask, jnp.exp(jax.nn.log_sigmoid(z) + after), 0.0)
        return jnp.einsum('bhqk,bkhd->bqhd', w.astype(v.dtype), v)

    return _unblock(lax.map(one, (qb, pb)))


def _diff_attention(q, k, v, q_pos, k_pos, lam, rel_bias):
    scale = HEAD_DIM ** -0.5
    k1, k2 = k[..., :HEAD_DIM], k[..., HEAD_DIM:]
    qb, pb = _query_blocks(q, q_pos)

    def one(args):
        qi, pi = args
        dist = pi[:, None] - k_pos[None, :]
        mask = dist >= 0
        bias = jnp.moveaxis(rel_bias.astype(jnp.float32)[_rel_bucket(dist)], -1, 0)[None]
        s1 = jnp.einsum('bqhd,bkhd->bhqk', qi[..., :HEAD_DIM], k1).astype(jnp.float32) * scale + bias
        s2 = jnp.einsum('bqhd,bkhd->bhqk', qi[..., HEAD_DIM:], k2).astype(jnp.float32) * scale + bias
        p1 = jax.nn.softmax(jnp.where(mask, s1, NEG_INF), axis=-1)
        p2 = jax.nn.softmax(jnp.where(mask, s2, NEG_INF), axis=-1)
        a = p1 - lam * p2
        return jnp.einsum('bhqk,bkhd->bqhd', a.astype(v.dtype), v)

    return _unblock(lax.map(one, (qb, pb)))


_SPLITS = [int(s) for s in np.cumsum([SB_W, SB_W, SB_W, DF_W, DF_W])]


def _layer(x, q_pos, k_pos, past, g, w_in, w_out, ffa, ffb, subln, lam, lam_init, rel_bias):
    b, t, _ = x.shape
    x = x + 0.5 * _rms(_swiglu(_rms(x, g[0]), *ffa), g[1])
    h = _rms(x, g[2])
    proj = h @ w_in
    sb_q, sb_k, sb_v, df_q, df_k, df_v = jnp.split(proj, _SPLITS, axis=-1)
    sb_q = sb_q.reshape(b, t, SB_HEADS, HEAD_DIM)
    sb_k = sb_k.reshape(b, t, SB_HEADS, HEAD_DIM)
    sb_v = sb_v.reshape(b, t, SB_HEADS, HEAD_DIM)
    df_q = df_q.reshape(b, t, DF_HEADS, 2 * HEAD_DIM)
    df_k = df_k.reshape(b, t, DF_HEADS, 2 * HEAD_DIM)
    df_v = df_v.reshape(b, t, DF_HEADS, 2 * HEAD_DIM)
    new_rows = (sb_k, sb_v, df_k, df_v)
    if past is None:
        ak, av, bk, bv = new_rows
    else:
        ak, av, bk, bv = [jnp.concatenate([p, n], axis=1) for p, n in zip(past, new_rows)]
    sb_o = _stick_breaking(sb_q, ak, av, q_pos, k_pos)
    df_o = _diff_attention(df_q, bk, bv, q_pos, k_pos, lam, rel_bias)
    df_o = _rms(df_o, subln) * (1.0 - lam_init)
    mix = jnp.concatenate([sb_o.reshape(b, t, SB_W), df_o.reshape(b, t, DF_W)], axis=-1) @ w_out
    x = x + _rms(mix, g[3])
    x = x + 0.5 * _rms(_swiglu(_rms(x, g[4]), *ffb), g[5])
    return x, new_rows


def setup_inputs(seed: int = 0) -> dict:
    key = jax.random.key(seed)
    ks = jax.random.split(key, 24)
    f32 = jnp.float32
    n_pages = PAST_LEN // PAGE_SIZE
    n_used = DEC_BATCH * n_pages
    n_phys = (n_used * 5 + 3) // 4
    perm = jax.random.permutation(ks[0], n_phys)
    page_table = perm[:n_used].reshape(DEC_BATCH, n_pages).astype(jnp.int32)
    nrm = lambda k, shape, s: jax.random.normal(k, shape, f32) * s
    return {
        "x_prompt": nrm(ks[1], (BATCH, SEQ, D_MODEL), 1.0),
        "x_sample": nrm(ks[2], (DEC_BATCH, DEC_SEQ, D_MODEL), 1.0),
        "cache_sb_k": nrm(ks[3], (DEPTH, n_phys, PAGE_SIZE, SB_HEADS, HEAD_DIM), 1.0),
        "cache_sb_v": nrm(ks[4], (DEPTH, n_phys, PAGE_SIZE, SB_HEADS, HEAD_DIM), 1.0),
        "cache_df_k": nrm(ks[5], (DEPTH, n_phys, PAGE_SIZE, DF_HEADS, 2 * HEAD_DIM), 1.0),
        "cache_df_v": nrm(ks[6], (DEPTH, n_phys, PAGE_SIZE, DF_HEADS, 2 * HEAD_DIM), 1.0),
        "page_table": page_table,
        "norm_gains": 1.0 + nrm(ks[7], (DEPTH, 6, D_MODEL), 0.05),
        "w_in": nrm(ks[8], (DEPTH, D_MODEL, IN_W), D_MODEL ** -0.5),
        "w_out": nrm(ks[9], (DEPTH, MIX_W, D_MODEL), MIX_W ** -0.5),
        "ffa_gate": nrm(ks[10], (DEPTH, D_MODEL, D_FF), D_MODEL ** -0.5),
        "ffa_up": nrm(ks[11], (DEPTH, D_MODEL, D_FF), D_MODEL ** -0.5),
        "ffa_down": nrm(ks[12], (DEPTH, D_FF, D_MODEL), D_FF ** -0.5),
        "ffb_gate": nrm(ks[13], (DEPTH, D_MODEL, D_FF), D_MODEL ** -0.5),
        "ffb_up": nrm(ks[14], (DEPTH, D_MODEL, D_FF), D_MODEL ** -0.5),
        "ffb_down": nrm(ks[15], (DEPTH, D_FF, D_MODEL), D_FF ** -0.5),
        "df_subln": 1.0 + nrm(ks[16], (DEPTH, 2 * HEAD_DIM), 0.05),
        "lambda_q1": nrm(ks[17], (DEPTH, HEAD_DIM), 0.1),
        "lambda_k1": nrm(ks[18], (DEPTH, HEAD_DIM), 0.1),
        "lambda_q2": nrm(ks[19], (DEPTH, HEAD_DIM), 0.1),
        "lambda_k2": nrm(ks[20], (DEPTH, HEAD_DIM), 0.1),
        "rel_bias": nrm(ks[21], (NUM_BUCKETS, DF_HEADS), 0.5),
    }


def reference(x_prompt, x_sample, cache_sb_k, cache_sb_v, cache_df_k, cache_df_v, page_table,
              norm_gains, w_in, w_out, ffa_gate, ffa_up, ffa_down, ffb_gate, ffb_up, ffb_down,
              df_subln, lambda_q1, lambda_k1, lambda_q2, lambda_k2, rel_bias):
    dec_b, n_pages = page_table.shape
    past_len = n_pages * cache_sb_k.shape[2]
    seq = x_prompt.shape[1]
    dec_seq = x_sample.shape[1]
    pos_p = jnp.arange(seq, dtype=jnp.int32)
    pos_sq = past_len + jnp.arange(dec_seq, dtype=jnp.int32)
    pos_sk = jnp.arange(past_len + dec_seq, dtype=jnp.int32)
    caches = (cache_sb_k, cache_sb_v, cache_df_k, cache_df_v)
    yp, ys = x_prompt, x_sample
    rows_p = ([], [], [], [])
    rows_s = ([], [], [], [])
    for l in range(DEPTH):
        lam_init = 0.8 - 0.6 * math.exp(-0.3 * l)
        lam = (jnp.exp(jnp.sum(lambda_q1[l].astype(jnp.float32) * lambda_k1[l].astype(jnp.float32)))
               - jnp.exp(jnp.sum(lambda_q2[l].astype(jnp.float32) * lambda_k2[l].astype(jnp.float32)))
               + lam_init)
        ffa = (ffa_gate[l], ffa_up[l], ffa_down[l])
        ffb = (ffb_gate[l], ffb_up[l], ffb_down[l])
        yp, new_p = _layer(yp, pos_p, pos_p, None, norm_gains[l], w_in[l], w_out[l], ffa, ffb,
                           df_subln[l], lam, lam_init, rel_bias)
        past = tuple(c[l, page_table].reshape(dec_b, past_len, *c.shape[3:]) for c in caches)
        ys, new_s = _layer(ys, pos_sq, pos_sk, past, norm_gains[l], w_in[l], w_out[l], ffa, ffb,
                           df_subln[l], lam, lam_init, rel_bias)
        for i in range(4):
            rows_p[i].append(new_p[i])
            rows_s[i].append(new_s[i])
    new_sb_k_prompt = jnp.stack(rows_p[0])
    new_sb_v_prompt = jnp.stack(rows_p[1])
    new_df_k_prompt = jnp.stack(rows_p[2])
    new_df_v_prompt = jnp.stack(rows_p[3])
    new_sb_k_sample = jnp.stack(rows_s[0])
    new_sb_v_sample = jnp.stack(rows_s[1])
    new_df_k_sample = jnp.stack(rows_s[2])
    new_df_v_sample = jnp.stack(rows_s[3])
    return (yp, ys, new_sb_k_prompt, new_sb_v_prompt, new_df_k_prompt, new_df_v_prompt,
            new_sb_k_sample, new_sb_v_sample, new_df_k_sample, new_df_v_sample)
```

```python
import functools
import math

import jax
import jax.numpy as jnp
from jax import lax
from jax.experimental import pallas as pl
from jax.experimental.pallas import tpu as pltpu

F32 = jnp.float32
BF16 = jnp.bfloat16

HEAD_DIM = 64
SB_HEADS = 8
DF_HEADS = 4
GROUP_W = SB_HEADS * HEAD_DIM
N_GROUPS = 6
NUM_BUCKETS = 32
MAX_EXACT = NUM_BUCKETS // 2
MAX_DISTANCE = 128
RMS_EPS = 1e-6
NEG_INF = -1e30
QK_SCALE = HEAD_DIM ** -0.5

LANES = 128
TOKEN_TILE = 512
ATTN_TILE = 256
FF_CHUNK = 256
VMEM_LIMIT_BYTES = 56 * 1024 * 1024


def _dot(a, b):
    return jnp.dot(a, b, preferred_element_type=F32)


def _dot_nt(a, b):
    return lax.dot_general(a, b, (((1,), (1,)), ((), ())), preferred_element_type=F32)


def _rms(x, g):
    ms = jnp.mean(x * x, axis=-1, keepdims=True)
    return x * lax.rsqrt(ms + RMS_EPS) * g


def _log_sigmoid(z):
    return jnp.minimum(z, 0.0) - jnp.log1p(jnp.exp(-jnp.abs(z)))


def _split_dot(x, u):
    hi = x.astype(BF16)
    lo = (x - hi.astype(F32)).astype(BF16)
    return _dot(hi, u) + _dot(lo, u)


def _swiglu_residual(x, g_pre, g_post, wg_ref, wu_ref, wd_ref):
    xn = _rms(x, g_pre).astype(BF16)
    d_ff = wg_ref.shape[1]
    acc = None
    for c0 in range(0, d_ff, FF_CHUNK):
        c1 = min(c0 + FF_CHUNK, d_ff)
        gate = _dot(xn, wg_ref[:, c0:c1])
        up = _dot(xn, wu_ref[:, c0:c1])
        h = (gate * jax.nn.sigmoid(gate) * up).astype(BF16)
        y = _dot(h, wd_ref[c0:c1, :])
        acc = y if acc is None else acc + y
    return x + 0.5 * _rms(acc, g_post)


def _pre_kernel(x_ref, g_ref, wg_ref, wu_ref, wd_ref, win_ref,
                x1_ref, qkv_ref, ksb_ref, vsb_ref, kdf_ref, vdf_ref):
    g = g_ref[...]
    x1 = _swiglu_residual(x_ref[...], g[0:1], g[1:2], wg_ref, wu_ref, wd_ref)
    x1_ref[...] = x1
    h = _rms(x1, g[2:3]).astype(BF16)
    f32_out = (None, ksb_ref, vsb_ref, None, kdf_ref, vdf_ref)
    for c in range(N_GROUPS):
        lo, hi = c * GROUP_W, (c + 1) * GROUP_W
        p = _dot(h, win_ref[:, lo:hi])
        if f32_out[c] is None:
            p = p * QK_SCALE
        else:
            f32_out[c][...] = p
        qkv_ref[:, lo:hi] = p.astype(BF16)


def _post_kernel(x_ref, osb_ref, odf_ref, g_ref, wout_ref, wg_ref, wu_ref, wd_ref, y_ref):
    g = g_ref[...]
    mix = _dot(osb_ref[...], wout_ref[0:GROUP_W, :]) + _dot(odf_ref[...], wout_ref[GROUP_W:, :])
    x2 = x_ref[...] + _rms(mix, g[3:4])
    y_ref[...] = _swiglu_residual(x2, g[4:5], g[5:6], wg_ref, wu_ref, wd_ref)


def _token_tile(n):
    t = TOKEN_TILE
    while n % t:
        t //= 2
    return t


def _resident(shape):
    nd = len(shape)
    return pl.BlockSpec(shape, lambda i: (0,) * nd, pipeline_mode=pl.Buffered(1))


def _rows(tm, width):
    return pl.BlockSpec((tm, width), lambda i: (i, 0))


def _dense_params():
    return pltpu.CompilerParams(dimension_semantics=("parallel",), vmem_limit_bytes=VMEM_LIMIT_BYTES)


def _pre_call(x, gains, wg, wu, wd, win):
    n, d = x.shape
    tm = _token_tile(n)
    kv = jax.ShapeDtypeStruct((n, GROUP_W), F32)
    return pl.pallas_call(
        _pre_kernel,
        grid=(n // tm,),
        in_specs=[_rows(tm, d), _resident(gains.shape), _resident(wg.shape), _resident(wu.shape),
                  _resident(wd.shape), _resident(win.shape)],
        out_specs=[_rows(tm, d), _rows(tm, N_GROUPS * GROUP_W)] + [_rows(tm, GROUP_W)] * 4,
        out_shape=[jax.ShapeDtypeStruct((n, d), F32),
                   jax.ShapeDtypeStruct((n, N_GROUPS * GROUP_W), BF16), kv, kv, kv, kv],
        compiler_params=_dense_params(),
        name="pre",
    )(x, gains, wg, wu, wd, win)


def _post_call(x, osb, odf, gains, wout, wg, wu, wd):
    n, d = x.shape
    tm = _token_tile(n)
    return pl.pallas_call(
        _post_kernel,
        grid=(n // tm,),
        in_specs=[_rows(tm, d), _rows(tm, GROUP_W), _rows(tm, GROUP_W), _resident(gains.shape),
                  _resident(wout.shape), _resident(wg.shape), _resident(wu.shape), _resident(wd.shape)],
        out_specs=_rows(tm, d),
        out_shape=jax.ShapeDtypeStruct((n, d), F32),
        compiler_params=_dense_params(),
        name="post",
    )(x, osb, odf, gains, wout, wg, wu, wd)


def _sb_prompt_kernel(q_ref, k_ref, v_ref, o_ref):
    i = pl.program_id(2)
    tq = q_ref.shape[0]
    r = lax.broadcasted_iota(jnp.int32, (tq, tq), 0)
    c = lax.broadcasted_iota(jnp.int32, (tq, tq), 1)
    later = (r > c).astype(BF16)
    diag_mask = c < r
    for h in range(2):
        lo, hi = h * HEAD_DIM, (h + 1) * HEAD_DIM
        q = q_ref[:, lo:hi]

        def tile(j, acc, run, mask):
            start = pl.multiple_of(j * tq, tq)
            kt = k_ref[pl.ds(start, tq), lo:hi]
            vt = v_ref[pl.ds(start, tq), lo:hi]
            z = _dot_nt(q, kt)
            ls = _log_sigmoid(z)
            lstay = ls - z
            if mask is not None:
                lstay = jnp.where(mask, lstay, 0.0)
            after = _split_dot(lstay, later) + run
            w = jnp.exp(ls + after)
            if mask is not None:
                w = jnp.where(mask, w, 0.0)
            acc = acc + _dot(w.astype(BF16), vt)
            run = run + jnp.sum(lstay, axis=-1, keepdims=True)
            return acc, run

        acc, run = tile(i, jnp.zeros((tq, HEAD_DIM), F32), jnp.zeros((tq, 1), F32), diag_mask)
        acc, run = lax.fori_loop(0, i, lambda t, s: tile(i - 1 - t, s[0], s[1], None), (acc, run))
        o_ref[:, lo:hi] = acc.astype(o_ref.dtype)


def _sb_prompt_call(qkv, batch, seq):
    tq = min(ATTN_TILE, seq)
    nq = seq // tq
    pairs = GROUP_W // LANES
    return pl.pallas_call(
        _sb_prompt_kernel,
        grid=(batch, pairs, nq),
        in_specs=[pl.BlockSpec((tq, LANES), lambda b, p, i: (b * nq + i, p)),
                  pl.BlockSpec((seq, LANES), lambda b, p, i: (b, pairs + p)),
                  pl.BlockSpec((seq, LANES), lambda b, p, i: (b, 2 * pairs + p))],
        out_specs=pl.BlockSpec((tq, LANES), lambda b, p, i: (b * nq + i, p)),
        out_shape=jax.ShapeDtypeStruct((batch * seq, GROUP_W), BF16),
        compiler_params=pltpu.CompilerParams(dimension_semantics=("parallel", "parallel", "parallel"),
                                             vmem_limit_bytes=VMEM_LIMIT_BYTES),
        name="sb_prompt",
    )(qkv, qkv, qkv)


def _df_prompt_kernel(sc_ref, q_ref, k_ref, v_ref, bias_ref, gain_ref, o_ref):
    head = pl.program_id(1)
    i = pl.program_id(2)
    tq = q_ref.shape[0]
    lam = sc_ref[0]
    post_scale = sc_ref[1]
    far_bias = sc_ref[2 + head]
    q1 = q_ref[:, :HEAD_DIM]
    q2 = q_ref[:, HEAD_DIM:]
    r = lax.broadcasted_iota(jnp.int32, (tq, tq), 0)
    c = lax.broadcasted_iota(jnp.int32, (tq, tq), 1)

    def update(state, s, vt):
        m, l, acc = state
        m_new = jnp.maximum(m, jnp.max(s, axis=-1, keepdims=True))
        alpha = jnp.exp(m - m_new)
        p = jnp.exp(s - m_new)
        l = alpha * l + jnp.sum(p, axis=-1, keepdims=True)
        acc = alpha * acc + _dot(p.astype(BF16), vt)
        return m_new, l, acc

    def tile(j, st1, st2, bias, mask):
        start = pl.multiple_of(j * tq, tq)
        kt = k_ref[pl.ds(start, tq), :]
        vt = v_ref[pl.ds(start, tq), :]
        s1 = _dot_nt(q1, kt[:, :HEAD_DIM]) + bias
        s2 = _dot_nt(q2, kt[:, HEAD_DIM:]) + bias
        if mask is not None:
            s1 = jnp.where(mask, s1, NEG_INF)
            s2 = jnp.where(mask, s2, NEG_INF)
        return update(st1, s1, vt), update(st2, s2, vt)

    init = (jnp.full((tq, 1), NEG_INF, F32), jnp.zeros((tq, 1), F32), jnp.zeros((tq, LANES), F32))
    st1, st2 = tile(i, init, init, bias_ref[0, 0], c <= r)
    has_prev = jnp.broadcast_to(i >= 1, (tq, tq))
    st1, st2 = tile(jnp.maximum(i - 1, 0), st1, st2, bias_ref[0, 1], has_prev)
    st1, st2 = lax.fori_loop(0, jnp.maximum(i - 1, 0),
                             lambda j, s: tile(j, s[0], s[1], far_bias, None), (st1, st2))
    o = st1[2] / st1[1] - lam * (st2[2] / st2[1])
    y = o * lax.rsqrt(jnp.mean(o * o, axis=-1, keepdims=True) + RMS_EPS) * gain_ref[...]
    o_ref[...] = (y * post_scale).astype(o_ref.dtype)


def _df_prompt_call(qkv, scalars, bias_tiles, gain, batch, seq):
    tq = bias_tiles.shape[-1]
    nq = seq // tq
    q0, k0, v0 = 3 * DF_HEADS, 4 * DF_HEADS, 5 * DF_HEADS
    return pl.pallas_call(
        _df_prompt_kernel,
        grid=(batch, DF_HEADS, nq),
        in_specs=[pl.BlockSpec(memory_space=pltpu.SMEM),
                  pl.BlockSpec((tq, LANES), lambda b, h, i: (b * nq + i, q0 + h)),
                  pl.BlockSpec((seq, LANES), lambda b, h, i: (b, k0 + h)),
                  pl.BlockSpec((seq, LANES), lambda b, h, i: (b, v0 + h)),
                  pl.BlockSpec((1, 2, tq, tq), lambda b, h, i: (h, 0, 0, 0)),
                  pl.BlockSpec((1, LANES), lambda b, h, i: (0, 0))],
        out_specs=pl.BlockSpec((tq, LANES), lambda b, h, i: (b * nq + i, h)),
        out_shape=jax.ShapeDtypeStruct((batch * seq, GROUP_W), BF16),
        compiler_params=pltpu.CompilerParams(dimension_semantics=("parallel", "parallel", "parallel"),
                                             vmem_limit_bytes=VMEM_LIMIT_BYTES),
        name="df_prompt",
    )(scalars, qkv, qkv, qkv, bias_tiles, gain)


def _padded_page(new_ref, page):
    rows = new_ref[0]
    pad = jnp.zeros((page - rows.shape[0], rows.shape[1]), F32)
    return jnp.concatenate([rows, pad], axis=0).astype(BF16)


def _fold_slots(acc, slot_width):
    row_slot = lax.broadcasted_iota(jnp.int32, acc.shape, 0) % 8
    col_head = lax.broadcasted_iota(jnp.int32, acc.shape, 1) // slot_width
    row_head = row_slot if slot_width == HEAD_DIM else row_slot // 2
    kept = jnp.where(col_head == row_head, acc, 0.0)
    return [jnp.sum(kept[8 * q:8 * q + 8], axis=0, keepdims=True) for q in range(acc.shape[0] // 8)]


def _sb_dec_kernel(n_pages, pt_ref, qbd_ref, knew_ref, vnew_ref, *refs):
    del pt_ref
    k_refs, v_refs, o_ref = refs[:n_pages], refs[n_pages:2 * n_pages], refs[2 * n_pages]
    page = k_refs[0].shape[2]
    qbd = qbd_ref[0]
    rows = qbd.shape[0]
    r = lax.broadcasted_iota(jnp.int32, (page, page), 0)
    c = lax.broadcasted_iota(jnp.int32, (page, page), 1)
    later = (r > c).astype(BF16)
    new_mask = (lax.broadcasted_iota(jnp.int32, (rows, page), 1)
                < lax.broadcasted_iota(jnp.int32, (rows, page), 0) // 8)
    acc = jnp.zeros((rows, qbd.shape[1]), F32)
    run = jnp.zeros((rows, 1), F32)
    for p in range(n_pages, -1, -1):
        if p == n_pages:
            kp, vp, mask = _padded_page(knew_ref, page), _padded_page(vnew_ref, page), new_mask
        else:
            kp, vp, mask = k_refs[p][0, 0].astype(BF16), v_refs[p][0, 0].astype(BF16), None
        z = _dot_nt(qbd, kp)
        ls = _log_sigmoid(z)
        lstay = ls - z
        if mask is not None:
            lstay = jnp.where(mask, lstay, 0.0)
        after = _split_dot(lstay, later) + run
        w = jnp.exp(ls + after)
        if mask is not None:
            w = jnp.where(mask, w, 0.0)
        acc = acc + _dot(w.astype(BF16), vp)
        run = run + jnp.sum(lstay, axis=-1, keepdims=True)
    for q, row in enumerate(_fold_slots(acc, HEAD_DIM)):
        o_ref[0, q:q + 1, :] = row


def _df_dec_kernel(n_pages, pt_ref, sc_ref, qbd_ref, knew_ref, vnew_ref, bias_ref, gain_ref, *refs):
    del pt_ref
    k_refs, v_refs, o_ref, s_ref = (refs[:n_pages], refs[n_pages:2 * n_pages], refs[2 * n_pages],
                                    refs[2 * n_pages + 1])
    page = k_refs[0].shape[2]
    qbd = qbd_ref[0]
    rows = qbd.shape[0]
    lam = sc_ref[0]
    post_scale = sc_ref[1]
    new_mask = (lax.broadcasted_iota(jnp.int32, (rows, page), 1)
                <= lax.broadcasted_iota(jnp.int32, (rows, page), 0) // 8)
    m = jnp.full((rows, 1), NEG_INF, F32)
    for p in range(n_pages + 1):
        kp = _padded_page(knew_ref, page) if p == n_pages else k_refs[p][0, 0].astype(BF16)
        s = _dot_nt(qbd, kp) + bias_ref[:, p * page:(p + 1) * page]
        if p == n_pages:
            s = jnp.where(new_mask, s, NEG_INF)
        s_ref[:, p * page:(p + 1) * page] = s
        m = jnp.maximum(m, jnp.max(s, axis=-1, keepdims=True))
    e = jnp.exp(s_ref[...] - m)
    l = jnp.sum(e, axis=-1, keepdims=True)
    second_map = lax.broadcasted_iota(jnp.int32, (rows, 1), 0) % 2 == 1
    coef = jnp.where(second_map, -lam, 1.0) / l
    a = (e * coef).astype(BF16)
    acc = jnp.zeros((rows, qbd.shape[1]), F32)
    for p in range(n_pages + 1):
        vp = _padded_page(vnew_ref, page) if p == n_pages else v_refs[p][0, 0].astype(BF16)
        acc = acc + _dot(a[:, p * page:(p + 1) * page], vp)
    gain = gain_ref[...]
    for q, row in enumerate(_fold_slots(acc, 2 * HEAD_DIM)):
        for h in range(DF_HEADS):
            seg = row[:, h * LANES:(h + 1) * LANES]
            y = seg * lax.rsqrt(jnp.mean(seg * seg, axis=-1, keepdims=True) + RMS_EPS) * gain
            o_ref[0, q:q + 1, h * LANES:(h + 1) * LANES] = y * post_scale


def _dec_call(kind, layer, page_table_flat, n_pages, qbd, knew, vnew, cache_k, cache_v, extras=()):
    n_samples, rows, width = qbd.shape
    page = cache_k.shape[2]
    dec_seq = rows // 8

    def page_spec(p):
        return pl.BlockSpec((1, 1, page, width), lambda b, pt: (layer, pt[b * n_pages + p], 0, 0))

    per_sample = lambda arr: pl.BlockSpec((1,) + arr.shape[1:], lambda b, pt: (b, 0, 0))
    pages = [page_spec(p) for p in range(n_pages)]
    common = [per_sample(qbd), per_sample(knew), per_sample(vnew)]
    if kind == "sb":
        body = functools.partial(_sb_dec_kernel, n_pages)
        in_specs, operands, scratch = common, [qbd, knew, vnew], []
    else:
        scalars, bias, gain = extras
        body = functools.partial(_df_dec_kernel, n_pages)
        in_specs = ([pl.BlockSpec(memory_space=pltpu.SMEM)] + common
                    + [pl.BlockSpec(bias.shape, lambda b, pt: (0, 0)),
                       pl.BlockSpec(gain.shape, lambda b, pt: (0, 0))])
        operands = [scalars, qbd, knew, vnew, bias, gain]
        scratch = [pltpu.VMEM((rows, (n_pages + 1) * page), F32)]
    return pl.pallas_call(
        body,
        grid_spec=pltpu.PrefetchScalarGridSpec(
            num_scalar_prefetch=1,
            grid=(n_samples,),
            in_specs=in_specs + pages + pages,
            out_specs=pl.BlockSpec((1, dec_seq, width), lambda b, pt: (b, 0, 0)),
            scratch_shapes=scratch),
        out_shape=jax.ShapeDtypeStruct((n_samples, dec_seq, width), F32),
        compiler_params=pltpu.CompilerParams(dimension_semantics=("parallel",),
                                             vmem_limit_bytes=VMEM_LIMIT_BYTES),
        name=kind + "_dec",
    )(page_table_flat, *operands, *([cache_k] * n_pages), *([cache_v] * n_pages))


def _rel_bucket(dist):
    n = jnp.maximum(dist, 0)
    nf = jnp.maximum(n, 1).astype(F32)
    large = MAX_EXACT + (jnp.log(nf / MAX_EXACT) / math.log(MAX_DISTANCE / MAX_EXACT)
                         * (NUM_BUCKETS - MAX_EXACT)).astype(jnp.int32)
    large = jnp.minimum(large, NUM_BUCKETS - 1)
    return jnp.where(n < MAX_EXACT, n, large)


def _prompt_bias_tiles(rel_bias, tq):
    r = jnp.arange(tq, dtype=jnp.int32)[:, None]
    c = jnp.arange(tq, dtype=jnp.int32)[None, :]
    dist = jnp.stack([r - c, tq + r - c])
    return jnp.moveaxis(rel_bias.astype(F32)[_rel_bucket(dist)], -1, 0)


def _dec_bias(rel_bias, past_len, page, dec_seq):
    rows = jnp.arange(8 * dec_seq, dtype=jnp.int32)
    q_pos = past_len + rows // 8
    head = (rows % 8) // 2
    k_pos = jnp.arange(past_len + page, dtype=jnp.int32)
    dist = q_pos[:, None] - k_pos[None, :]
    return rel_bias.astype(F32)[_rel_bucket(dist), head[:, None]]


def _block_diag_queries(q):
    n, t, w = q.shape
    slots = w // HEAD_DIM
    q4 = q.reshape(n, t, 1, slots, HEAD_DIM)
    keep = jnp.eye(slots, dtype=bool)[None, None, :, :, None]
    return jnp.where(keep, q4, jnp.zeros((), q.dtype)).reshape(n, t * slots, w)


def _pad_rows(x, rows):
    return jnp.pad(x, ((0, 0), (0, rows - x.shape[1]), (0, 0)))


def kernel(x_prompt, x_sample, cache_sb_k, cache_sb_v, cache_df_k, cache_df_v, page_table, norm_gains, w_in, w_out, ffa_gate, ffa_up, ffa_down, ffb_gate, ffb_up, ffb_down, df_subln, lambda_q1, lambda_k1, lambda_q2, lambda_k2, rel_bias):
    batch, seq, d_model = x_prompt.shape
    dec_batch, dec_seq, _ = x_sample.shape
    depth, n_phys, page = cache_sb_k.shape[:3]
    n_pages = page_table.shape[1]
    past_len = n_pages * page
    n_prompt, n_sample = batch * seq, dec_batch * dec_seq
    assert dec_seq <= 8 and seq % min(ATTN_TILE, seq) == 0

    x = jnp.concatenate([x_prompt.reshape(n_prompt, d_model), x_sample.reshape(n_sample, d_model)])
    caches = [c.reshape(depth, n_phys, page, GROUP_W) for c in (cache_sb_k, cache_sb_v, cache_df_k, cache_df_v)]
    pt_flat = page_table.reshape(-1).astype(jnp.int32)
    to16 = lambda w: w.astype(BF16)
    w_in16, w_out16 = to16(w_in), to16(w_out)
    ffa = [to16(w) for w in (ffa_gate, ffa_up, ffa_down)]
    ffb = [to16(w) for w in (ffb_gate, ffb_up, ffb_down)]

    tq = min(ATTN_TILE, seq)
    bias_tiles = _prompt_bias_tiles(rel_bias, tq)
    bias_dec = _dec_bias(rel_bias, past_len, page, dec_seq)
    far_bias = rel_bias.astype(F32)[NUM_BUCKETS - 1]

    new_rows = [[] for _ in range(4)]
    for l in range(depth):
        lam_init = 0.8 - 0.6 * math.exp(-0.3 * l)
        lam = (jnp.exp(jnp.sum(lambda_q1[l].astype(F32) * lambda_k1[l].astype(F32)))
               - jnp.exp(jnp.sum(lambda_q2[l].astype(F32) * lambda_k2[l].astype(F32)))
               + lam_init)
        scalars = jnp.concatenate([jnp.stack([lam, jnp.asarray(1.0 - lam_init, F32)]), far_bias]).astype(F32)
        gain = df_subln[l].astype(F32).reshape(1, 2 * HEAD_DIM)
        gains = norm_gains[l].astype(F32)

        x1, qkv, ksb, vsb, kdf, vdf = _pre_call(x, gains, ffa[0][l], ffa[1][l], ffa[2][l], w_in16[l])
        for rows, arr in zip(new_rows, (ksb, vsb, kdf, vdf)):
            rows.append(arr)

        osb_p = _sb_prompt_call(qkv, batch, seq)
        odf_p = _df_prompt_call(qkv, scalars, bias_tiles, gain, batch, seq)

        qkv_s = qkv[n_prompt:].reshape(dec_batch, dec_seq, N_GROUPS * GROUP_W)
        new = lambda arr: _pad_rows(arr[n_prompt:].reshape(dec_batch, dec_seq, GROUP_W), 8)
        osb_s = _dec_call("sb", l, pt_flat, n_pages, _block_diag_queries(qkv_s[..., :GROUP_W]),
                          new(ksb), new(vsb), caches[0], caches[1])
        odf_s = _dec_call("df", l, pt_flat, n_pages,
                          _block_diag_queries(qkv_s[..., 3 * GROUP_W:4 * GROUP_W]),
                          new(kdf), new(vdf), caches[2], caches[3], (scalars, bias_dec, gain))

        osb = jnp.concatenate([osb_p, osb_s.reshape(n_sample, GROUP_W).astype(BF16)])
        odf = jnp.concatenate([odf_p, odf_s.reshape(n_sample, GROUP_W).astype(BF16)])
        x = _post_call(x1, osb, odf, gains, w_out16[l], ffb[0][l], ffb[1][l], ffb[2][l])

    y_prompt = x[:n_prompt].reshape(batch, seq, d_model)
    y_sample = x[n_prompt:].reshape(dec_batch, dec_seq, d_model)
    shapes = ((SB_HEADS, HEAD_DIM), (SB_HEADS, HEAD_DIM), (DF_HEADS, 2 * HEAD_DIM), (DF_HEADS, 2 * HEAD_DIM))
    prompt_rows = [jnp.stack([a[:n_prompt].reshape(batch, seq, *s) for a in rows])
                   for rows, s in zip(new_rows, shapes)]
    sample_rows = [jnp.stack([a[n_prompt:].reshape(dec_batch, dec_seq, *s) for a in rows])
                   for rows, s in zip(new_rows, shapes)]
    return (y_prompt, y_sample, *prompt_rows, *sample_rows)
```

```python
import functools
import math

import jax
import jax.numpy as jnp
from jax import lax
from jax.experimental import pallas as pl
from jax.experimental.pallas import tpu as pltpu

F32 = jnp.float32
BF16 = jnp.bfloat16

HEAD_DIM = 64
SB_HEADS = 8
DF_HEADS = 4
GROUP_W = SB_HEADS * HEAD_DIM
N_GROUPS = 6
NUM_BUCKETS = 32
MAX_EXACT = NUM_BUCKETS // 2
MAX_DISTANCE = 128
RMS_EPS = 1e-6
NEG_INF = -1e30
QK_SCALE = HEAD_DIM ** -0.5

LANES = 128
SUBLANES = 8
TOKEN_TILE = 512
ATTN_TILE = 256
ATTN_BLOCK_W = 256
SB_PHASE_GROUP = 2
FF_CHUNK = 256
VMEM_LIMIT_BYTES = 56 * 1024 * 1024


def _dot(a, b):
    return jnp.dot(a, b, preferred_element_type=F32)


def _dot_nt(a, b):
    return lax.dot_general(a, b, (((1,), (1,)), ((), ())), preferred_element_type=F32)


def _rms(x, g):
    ms = jnp.mean(x * x, axis=-1, keepdims=True)
    return x * lax.rsqrt(ms + RMS_EPS) * g


def _log_sigmoid(z):
    return jnp.minimum(z, 0.0) - jnp.log(1.0 + jnp.exp(-jnp.abs(z)))


def _split_dot(x, u):
    n = x.shape[0]
    hi = x.astype(BF16)
    lo = (x - hi.astype(F32)).astype(BF16)
    both = _dot(jnp.concatenate([hi, lo], axis=0), u)
    return both[:n] + both[n:]


def _later_matrix(n):
    r = lax.broadcasted_iota(jnp.int32, (n, n), 0)
    c = lax.broadcasted_iota(jnp.int32, (n, n), 1)
    return (r > c).astype(BF16)


def _stick_breaking_tile(z, later, run, mask):
    ls = _log_sigmoid(z)
    lstay = ls - z
    if mask is not None:
        lstay = jnp.where(mask, lstay, 0.0)
    after = _split_dot(lstay, later) + run
    w = jnp.exp(ls + after)
    if mask is not None:
        w = jnp.where(mask, w, 0.0)
    return w, run + jnp.sum(lstay, axis=-1, keepdims=True)


def _swiglu_residual(x, g_pre, g_post, wg_ref, wu_ref, wd_ref):
    xn = _rms(x, g_pre).astype(BF16)
    d_ff = wg_ref.shape[1]
    acc = None
    for c0 in range(0, d_ff, FF_CHUNK):
        c1 = min(c0 + FF_CHUNK, d_ff)
        gate = _dot(xn, wg_ref[:, c0:c1])
        up = _dot(xn, wu_ref[:, c0:c1])
        h = (gate * jax.nn.sigmoid(gate) * up).astype(BF16)
        y = _dot(h, wd_ref[c0:c1, :])
        acc = y if acc is None else acc + y
    return x + 0.5 * _rms(acc, g_post)


def _pre_kernel(x_ref, g_ref, wg_ref, wu_ref, wd_ref, win_ref,
                x1_ref, qkv_ref, ksb_ref, vsb_ref, kdf_ref, vdf_ref):
    g = g_ref[...]
    x1 = _swiglu_residual(x_ref[...], g[0:1], g[1:2], wg_ref, wu_ref, wd_ref)
    x1_ref[...] = x1
    h = _rms(x1, g[2:3]).astype(BF16)
    f32_out = (None, ksb_ref, vsb_ref, None, kdf_ref, vdf_ref)
    for c in range(N_GROUPS):
        lo, hi = c * GROUP_W, (c + 1) * GROUP_W
        p = _dot(h, win_ref[:, lo:hi])
        if f32_out[c] is None:
            p = p * QK_SCALE
        else:
            f32_out[c][...] = p
        qkv_ref[:, lo:hi] = p.astype(BF16)


def _post_kernel(x_ref, osb_ref, odf_ref, g_ref, wout_ref, wg_ref, wu_ref, wd_ref, y_ref):
    g = g_ref[...]
    mix = _dot(osb_ref[...], wout_ref[0:GROUP_W, :]) + _dot(odf_ref[...], wout_ref[GROUP_W:, :])
    x2 = x_ref[...] + _rms(mix, g[3:4])
    y_ref[...] = _swiglu_residual(x2, g[4:5], g[5:6], wg_ref, wu_ref, wd_ref)


def _token_tile(n):
    t = TOKEN_TILE
    while n % t:
        t //= 2
    return t


def _resident(shape):
    nd = len(shape)
    return pl.BlockSpec(shape, lambda i: (0,) * nd, pipeline_mode=pl.Buffered(1))


def _rows(tm, width):
    return pl.BlockSpec((tm, width), lambda i: (i, 0))


def _params(n_axes):
    return pltpu.CompilerParams(dimension_semantics=("parallel",) * n_axes, vmem_limit_bytes=VMEM_LIMIT_BYTES)


def _pre_call(x, gains, wg, wu, wd, win):
    n, d = x.shape
    tm = _token_tile(n)
    kv = jax.ShapeDtypeStruct((n, GROUP_W), F32)
    return pl.pallas_call(
        _pre_kernel,
        grid=(n // tm,),
        in_specs=[_rows(tm, d), _resident(gains.shape), _resident(wg.shape), _resident(wu.shape),
                  _resident(wd.shape), _resident(win.shape)],
        out_specs=[_rows(tm, d), _rows(tm, N_GROUPS * GROUP_W)] + [_rows(tm, GROUP_W)] * 4,
        out_shape=[jax.ShapeDtypeStruct((n, d), F32),
                   jax.ShapeDtypeStruct((n, N_GROUPS * GROUP_W), BF16), kv, kv, kv, kv],
        compiler_params=_params(1),
        name="pre",
    )(x, gains, wg, wu, wd, win)


def _post_call(x, osb, odf, gains, wout, wg, wu, wd):
    n, d = x.shape
    tm = _token_tile(n)
    return pl.pallas_call(
        _post_kernel,
        grid=(n // tm,),
        in_specs=[_rows(tm, d), _rows(tm, GROUP_W), _rows(tm, GROUP_W), _resident(gains.shape),
                  _resident(wout.shape), _resident(wg.shape), _resident(wu.shape), _resident(wd.shape)],
        out_specs=_rows(tm, d),
        out_shape=jax.ShapeDtypeStruct((n, d), F32),
        compiler_params=_params(1),
        name="post",
    )(x, osb, odf, gains, wout, wg, wu, wd)


def _sb_prompt_kernel(q_ref, k_ref, v_ref, o_ref):
    i = pl.program_id(2)
    tq, width = q_ref.shape
    heads = width // HEAD_DIM
    later = _later_matrix(tq)
    r = lax.broadcasted_iota(jnp.int32, (tq, tq), 0)
    c = lax.broadcasted_iota(jnp.int32, (tq, tq), 1)
    diag_mask = c < r
    lanes = [slice(h * HEAD_DIM, (h + 1) * HEAD_DIM) for h in range(heads)]
    qs = [q_ref[:, sl] for sl in lanes]
    groups = [list(range(g, g + SB_PHASE_GROUP)) for g in range(0, heads, SB_PHASE_GROUP)]

    def scores(rows, hs):
        return [_dot_nt(qs[h], k_ref[rows, lanes[h]]) for h in hs]

    def log_terms(zs, mask):
        out = []
        for z in zs:
            ls = _log_sigmoid(z)
            lstay = ls - z
            if mask is not None:
                lstay = jnp.where(mask, lstay, 0.0)
            hi = lstay.astype(BF16)
            out.append((ls, lstay, hi, (lstay - hi.astype(F32)).astype(BF16)))
        return out

    def cumsums(terms):
        n = len(terms)
        local = _dot(jnp.concatenate([t[2] for t in terms] + [t[3] for t in terms], axis=0), later)
        return [local[k * tq:(k + 1) * tq] + local[(n + k) * tq:(n + k + 1) * tq] for k in range(n)]

    def weights(terms, local, runs, mask):
        ws, new_runs = [], []
        for (ls, lstay, _, _), loc, run in zip(terms, local, runs):
            w = jnp.exp(ls + loc + run)
            if mask is not None:
                w = jnp.where(mask, w, 0.0)
            ws.append(w.astype(BF16))
            new_runs.append(run + jnp.sum(lstay, axis=-1, keepdims=True))
        return ws, new_runs

    def values(rows, hs, ws, accs):
        return [acc + _dot(w, v_ref[rows, lanes[h]]) for h, w, acc in zip(hs, ws, accs)]

    def tile(j, state, mask):
        rows = pl.ds(pl.multiple_of(j * tq, tq), tq)
        n = len(groups)
        accs = [[state[h][0] for h in hs] for hs in groups]
        runs = [[state[h][1] for h in hs] for hs in groups]
        z, terms, local, ws = [None] * n, [None] * n, [None] * n, [None] * n
        for step in range(n + 4):
            for g, hs in enumerate(groups):
                phase = step - g
                if phase == 0:
                    z[g] = scores(rows, hs)
                elif phase == 1:
                    terms[g] = log_terms(z[g], mask)
                elif phase == 2:
                    local[g] = cumsums(terms[g])
                elif phase == 3:
                    ws[g], runs[g] = weights(terms[g], local[g], runs[g], mask)
                elif phase == 4:
                    accs[g] = values(rows, hs, ws[g], accs[g])
        out = [None] * heads
        for g, hs in enumerate(groups):
            for k, h in enumerate(hs):
                out[h] = (accs[g][k], runs[g][k])
        return tuple(out)

    state = tuple((jnp.zeros((tq, HEAD_DIM), F32), jnp.zeros((tq, 1), F32)) for _ in range(heads))
    state = tile(i, state, diag_mask)
    state = lax.fori_loop(0, i, lambda t, s: tile(i - 1 - t, s, None), state)
    for h in range(heads):
        o_ref[:, lanes[h]] = state[h][0].astype(o_ref.dtype)


def _df_prompt_kernel(sc_ref, q_ref, k_ref, v_ref, bias_ref, gain_ref, o_ref):
    block = pl.program_id(1)
    i = pl.program_id(2)
    tq, width = q_ref.shape
    heads = width // LANES
    lam = sc_ref[0]
    post_scale = sc_ref[1]
    r = lax.broadcasted_iota(jnp.int32, (tq, tq), 0)
    c = lax.broadcasted_iota(jnp.int32, (tq, tq), 1)
    maps = [(h, m) for h in range(heads) for m in range(2)]
    lane = lambda h, m: slice(h * LANES + m * HEAD_DIM, h * LANES + (m + 1) * HEAD_DIM)
    qs = {hm: q_ref[:, lane(*hm)] for hm in maps}

    def scores(rows, h, bias_of, mask):
        out = []
        for m in range(2):
            s = _dot_nt(qs[(h, m)], k_ref[rows, lane(h, m)]) + bias_of(h)
            out.append(s if mask is None else jnp.where(mask, s, NEG_INF))
        return out

    def softmax_terms(ss, old):
        out = []
        for s, (mx, l, _) in zip(ss, old):
            m_new = jnp.maximum(mx, jnp.max(s, axis=-1, keepdims=True))
            alpha = jnp.exp(mx - m_new)
            p = jnp.exp(s - m_new)
            out.append((m_new, alpha * l + jnp.sum(p, axis=-1, keepdims=True), alpha, p.astype(BF16)))
        return out

    def values(rows, h, terms, old):
        vt = v_ref[rows, h * LANES:(h + 1) * LANES]
        return [(m_new, l, alpha * acc + _dot(p, vt)) for (m_new, l, alpha, p), (_, _, acc) in zip(terms, old)]

    def tile(j, state, bias_of, mask):
        rows = pl.ds(pl.multiple_of(j * tq, tq), tq)
        old = [[state[(h, m)] for m in range(2)] for h in range(heads)]
        ss, terms, new = [None] * heads, [None] * heads, [None] * heads
        for step in range(heads + 2):
            for h in range(heads):
                phase = step - h
                if phase == 0:
                    ss[h] = scores(rows, h, bias_of, mask)
                elif phase == 1:
                    terms[h] = softmax_terms(ss[h], old[h])
                elif phase == 2:
                    new[h] = values(rows, h, terms[h], old[h])
        return {(h, m): new[h][m] for h, m in maps}

    init = (jnp.full((tq, 1), NEG_INF, F32), jnp.zeros((tq, 1), F32), jnp.zeros((tq, LANES), F32))
    state = tile(i, {hm: init for hm in maps}, lambda h: bias_ref[h, 0], c <= r)
    has_prev = jnp.broadcast_to(i >= 1, (tq, tq))
    state = tile(jnp.maximum(i - 1, 0), state, lambda h: bias_ref[h, 1], has_prev)
    state = lax.fori_loop(0, jnp.maximum(i - 1, 0),
                          lambda j, s: tile(j, s, lambda h: sc_ref[2 + block * heads + h], None), state)
    for h in range(heads):
        (_, l1, a1), (_, l2, a2) = state[(h, 0)], state[(h, 1)]
        o = a1 / l1 - lam * (a2 / l2)
        y = o * lax.rsqrt(jnp.mean(o * o, axis=-1, keepdims=True) + RMS_EPS) * gain_ref[...]
        o_ref[:, h * LANES:(h + 1) * LANES] = (y * post_scale).astype(o_ref.dtype)


def _prompt_attention_call(body, name, qkv, group, batch, seq, tq, extra_operands=(), extra_specs=()):
    nq = seq // tq
    blocks = GROUP_W // ATTN_BLOCK_W
    col = lambda g: (lambda b, p, i: (b, g * blocks + p))
    q_spec = pl.BlockSpec((tq, ATTN_BLOCK_W), lambda b, p, i: (b * nq + i, group * blocks + p))
    kv_spec = lambda g: pl.BlockSpec((seq, ATTN_BLOCK_W), col(g))
    return pl.pallas_call(
        body,
        grid=(batch, blocks, nq),
        in_specs=list(extra_specs[:1]) + [q_spec, kv_spec(group + 1), kv_spec(group + 2)] + list(extra_specs[1:]),
        out_specs=pl.BlockSpec((tq, ATTN_BLOCK_W), lambda b, p, i: (b * nq + i, p)),
        out_shape=jax.ShapeDtypeStruct((batch * seq, GROUP_W), BF16),
        compiler_params=_params(3),
        name=name,
    )(*extra_operands[:1], qkv, qkv, qkv, *extra_operands[1:])


def _sb_prompt_call(qkv, batch, seq, tq):
    return _prompt_attention_call(_sb_prompt_kernel, "sb_prompt", qkv, 0, batch, seq, tq)


def _df_prompt_call(qkv, scalars, bias_tiles, gain, batch, seq, tq):
    heads = ATTN_BLOCK_W // LANES
    specs = [pl.BlockSpec(memory_space=pltpu.SMEM),
             pl.BlockSpec((heads, 2, tq, tq), lambda b, p, i: (p, 0, 0, 0)),
             pl.BlockSpec((1, LANES), lambda b, p, i: (0, 0))]
    return _prompt_attention_call(_df_prompt_kernel, "df_prompt", qkv, 3, batch, seq, tq,
                                  (scalars, bias_tiles, gain), specs)


def _new_page(new_ref, lanes, page):
    rows = new_ref[0][:, lanes]
    pad = jnp.zeros((page - rows.shape[0], rows.shape[1]), F32)
    return jnp.concatenate([rows, pad], axis=0).astype(BF16)


def _head_page(refs, new_ref, p, h, heads, page, lanes):
    if p == len(refs):
        return _new_page(new_ref, lanes, page)
    return refs[p][0, 0, pl.ds(h, page, stride=heads), :].astype(BF16)


def _sb_dec_kernel(n_pages, pt_ref, q_ref, knew_ref, vnew_ref, *refs):
    del pt_ref
    k_refs, v_refs = refs[:n_pages], refs[n_pages:2 * n_pages]
    o_ref, s_ref = refs[2 * n_pages:]
    rows, width = q_ref.shape[1:]
    page = k_refs[0].shape[3]
    blocks = n_pages + 1
    qrows = rows // SB_HEADS
    key_block = lambda p: slice(p * page, (p + 1) * page)
    page_of = lambda pages, new_ref, p: (new_ref[0] if p == n_pages else pages[p][0, 0]).astype(BF16)
    q = q_ref[0].astype(BF16)
    for p in range(blocks):
        s_ref[:, key_block(p)] = _dot(q, page_of(k_refs, knew_ref, p))
    z = s_ref[...]
    lane = lax.broadcasted_iota(jnp.int32, z.shape, 1)
    query = lax.broadcasted_iota(jnp.int32, z.shape, 0) % qrows
    mask = lane < n_pages * page + query
    ls = _log_sigmoid(z)
    lstay = jnp.where(mask, ls - z, 0.0)
    hi = lstay.astype(BF16)
    lo = (lstay - hi.astype(F32)).astype(BF16)
    stacked = jnp.concatenate([part[:, key_block(p)] for part in (hi, lo) for p in range(blocks)], axis=0)
    local = _dot(stacked, _later_matrix(page))
    run = jnp.zeros((rows, 1), F32)
    acc = jnp.zeros((rows, width), F32)
    for p in range(blocks - 1, -1, -1):
        after = local[p * rows:(p + 1) * rows] + local[(blocks + p) * rows:(blocks + p + 1) * rows] + run
        w = jnp.where(mask[:, key_block(p)], jnp.exp(ls[:, key_block(p)] + after), 0.0)
        acc = acc + _dot_nt(w.astype(BF16), page_of(v_refs, vnew_ref, p))
        run = run + jnp.sum(lstay[:, key_block(p)], axis=-1, keepdims=True)
    for h in range(SB_HEADS):
        lanes = slice(h * HEAD_DIM, (h + 1) * HEAD_DIM)
        o_ref[0, :, lanes] = acc[h * qrows:(h + 1) * qrows, lanes]


def _df_dec_kernel(n_pages, pt_ref, sc_ref, q_ref, knew_ref, vnew_ref, bias_ref, gain_ref, *refs):
    del pt_ref
    k_refs, v_refs = refs[:n_pages], refs[n_pages:2 * n_pages]
    o_ref, s_ref = refs[2 * n_pages:]
    heads, hrows, width = q_ref.shape[1:]
    qrows = hrows // 2
    page = k_refs[0].shape[2] // heads
    blocks = n_pages + 1
    lam = sc_ref[0]
    post_scale = sc_ref[1]
    key_block = lambda p: slice(p * page, (p + 1) * page)
    head_rows = lambda h: slice(h * hrows, (h + 1) * hrows)
    head_lanes = lambda h: slice(h * width, (h + 1) * width)
    for h in range(heads):
        qh = q_ref[0, h].astype(BF16)
        for p in range(blocks):
            s_ref[head_rows(h), key_block(p)] = (
                _dot_nt(qh, _head_page(k_refs, knew_ref, p, h, heads, page, head_lanes(h)))
                + bias_ref[head_rows(h), key_block(p)])
    rows = heads * hrows
    new_mask = (lax.broadcasted_iota(jnp.int32, (rows, page), 1)
                <= lax.broadcasted_iota(jnp.int32, (rows, page), 0) % qrows)
    s_ref[:, key_block(n_pages)] = jnp.where(new_mask, s_ref[:, key_block(n_pages)], NEG_INF)
    s = s_ref[...]
    e = jnp.exp(s - jnp.max(s, axis=-1, keepdims=True))
    second_map = lax.broadcasted_iota(jnp.int32, (rows, 1), 0) % hrows >= qrows
    coef = jnp.where(second_map, -lam, 1.0) / jnp.sum(e, axis=-1, keepdims=True)
    s_ref[...] = e * coef
    gain = gain_ref[...]
    for h in range(heads):
        acc = jnp.zeros((hrows, width), F32)
        for p in range(blocks):
            acc = acc + _dot(s_ref[head_rows(h), key_block(p)].astype(BF16),
                             _head_page(v_refs, vnew_ref, p, h, heads, page, head_lanes(h)))
        o = acc[:qrows] + acc[qrows:]
        y = o * lax.rsqrt(jnp.mean(o * o, axis=-1, keepdims=True) + RMS_EPS) * gain
        o_ref[0, :, head_lanes(h)] = y * post_scale


def _dec_call(kind, layer, page_table_flat, n_pages, q, knew, vnew, cache_k, cache_v, extras=()):
    n_samples = q.shape[0]
    page_block = (1, 1) + cache_k.shape[2:]

    def page_spec(p):
        return pl.BlockSpec(page_block, lambda b, pt: (layer, pt[b * n_pages + p], 0, 0))

    per_sample = lambda arr: pl.BlockSpec((1,) + arr.shape[1:], lambda b, pt: (b,) + (0,) * (arr.ndim - 1))
    whole = lambda arr: pl.BlockSpec(arr.shape, lambda b, pt: (0,) * arr.ndim)
    pages = [page_spec(p) for p in range(n_pages)]
    common = [per_sample(q), per_sample(knew), per_sample(vnew)]
    if kind == "sb":
        body = functools.partial(_sb_dec_kernel, n_pages)
        in_specs, operands = common, [q, knew, vnew]
        scratch = [pltpu.VMEM((q.shape[1], (n_pages + 1) * cache_k.shape[3]), F32)]
    else:
        scalars, bias, gain = extras
        body = functools.partial(_df_dec_kernel, n_pages)
        in_specs = [pl.BlockSpec(memory_space=pltpu.SMEM)] + common + [whole(bias), whole(gain)]
        operands = [scalars, q, knew, vnew, bias, gain]
        scratch = [pltpu.VMEM(bias.shape, F32)]
    return pl.pallas_call(
        body,
        grid_spec=pltpu.PrefetchScalarGridSpec(
            num_scalar_prefetch=1,
            grid=(n_samples,),
            in_specs=in_specs + pages + pages,
            out_specs=pl.BlockSpec((1, SUBLANES, GROUP_W), lambda b, pt: (b, 0, 0)),
            scratch_shapes=scratch),
        out_shape=jax.ShapeDtypeStruct((n_samples, SUBLANES, GROUP_W), F32),
        compiler_params=_params(1),
        name=kind + "_dec",
    )(page_table_flat, *operands, *([cache_k] * n_pages), *([cache_v] * n_pages))


def _rel_bucket(dist):
    n = jnp.maximum(dist, 0)
    nf = jnp.maximum(n, 1).astype(F32)
    large = MAX_EXACT + (jnp.log(nf / MAX_EXACT) / math.log(MAX_DISTANCE / MAX_EXACT)
                         * (NUM_BUCKETS - MAX_EXACT)).astype(jnp.int32)
    large = jnp.minimum(large, NUM_BUCKETS - 1)
    return jnp.where(n < MAX_EXACT, n, large)


def _bias_of_dist(rel_bias, dist):
    return rel_bias.astype(F32)[_rel_bucket(dist)].T


def _toeplitz(v, n):
    heads, period = v.shape
    return jnp.tile(v, (1, n))[:, :n * (period - 1)].reshape(heads, n, period - 1)[:, :, :n]


def _prompt_bias_tiles(rel_bias, tq):
    m = jnp.arange(2 * tq, dtype=jnp.int32)
    diag = jnp.where(m < tq, -m, 2 * tq - m)
    prev = jnp.where(m < tq, tq - m, 3 * tq - m)
    return jnp.stack([_toeplitz(_bias_of_dist(rel_bias, diag), tq),
                      _toeplitz(_bias_of_dist(rel_bias, prev), tq)], axis=1)


def _dec_bias(rel_bias, past_len, n_keys, qrows):
    u = _bias_of_dist(rel_bias, past_len + (qrows - 1) - jnp.arange(n_keys + qrows - 1, dtype=jnp.int32))
    per_q = jnp.stack([u[:, qrows - 1 - q:qrows - 1 - q + n_keys] for q in range(qrows)], axis=1)
    heads = per_q.shape[0]
    return jnp.broadcast_to(per_q[:, None], (heads, 2, qrows, n_keys)).reshape(heads * 2 * qrows, n_keys)


def _pad_axis(x, axis, size):
    pad = [(0, 0)] * x.ndim
    pad[axis] = (0, size - x.shape[axis])
    return jnp.pad(x, pad)


def _sb_dec_queries(q):
    n, t, w = q.shape
    qh = _pad_axis(q.astype(F32).reshape(n, 1, t, SB_HEADS, HEAD_DIM), 2, SUBLANES)
    keep = jnp.eye(SB_HEADS, dtype=bool)[None, :, None, :, None]
    return jnp.where(keep, qh, 0.0).reshape(n, SB_HEADS * SUBLANES, w)


def _sb_dec_new_rows(rows, page):
    return _pad_axis(rows.transpose(0, 2, 1), 2, page)


def _df_dec_queries(q):
    n, t, _ = q.shape
    qh = q.astype(F32).reshape(n, t, DF_HEADS, 2, HEAD_DIM).transpose(0, 2, 3, 1, 4)
    qh = _pad_axis(qh, 3, SUBLANES)[:, :, :, :, None, :]
    keep = jnp.eye(2, dtype=bool)[None, None, :, None, :, None]
    return jnp.where(keep, qh, 0.0).reshape(n, DF_HEADS, 2 * SUBLANES, 2 * HEAD_DIM)


def kernel(x_prompt, x_sample, cache_sb_k, cache_sb_v, cache_df_k, cache_df_v, page_table, norm_gains, w_in, w_out, ffa_gate, ffa_up, ffa_down, ffb_gate, ffb_up, ffb_down, df_subln, lambda_q1, lambda_k1, lambda_q2, lambda_k2, rel_bias):
    batch, seq, d_model = x_prompt.shape
    dec_batch, dec_seq, _ = x_sample.shape
    depth, n_phys, page = cache_sb_k.shape[:3]
    n_pages = page_table.shape[1]
    past_len = n_pages * page
    n_prompt, n_sample = batch * seq, dec_batch * dec_seq
    tq = min(ATTN_TILE, seq)
    assert dec_seq <= SUBLANES and seq % tq == 0 and tq > MAX_DISTANCE

    x = jnp.concatenate([x_prompt.reshape(n_prompt, d_model), x_sample.reshape(n_sample, d_model)])
    caches = [jnp.transpose(c, (0, 1, 3, 4, 2)).reshape(depth, n_phys, GROUP_W, page)
              for c in (cache_sb_k, cache_sb_v)]
    caches += [c.reshape(depth, n_phys, page * DF_HEADS, 2 * HEAD_DIM) for c in (cache_df_k, cache_df_v)]
    pt_flat = page_table.reshape(-1).astype(jnp.int32)
    to16 = lambda w: w.astype(BF16)
    w_in16, w_out16 = to16(w_in), to16(w_out)
    ffa = [to16(w) for w in (ffa_gate, ffa_up, ffa_down)]
    ffb = [to16(w) for w in (ffb_gate, ffb_up, ffb_down)]

    bias_tiles = _prompt_bias_tiles(rel_bias, tq)
    bias_dec = _dec_bias(rel_bias, past_len, past_len + page, SUBLANES)
    far_bias = rel_bias.astype(F32)[NUM_BUCKETS - 1]

    new_rows = [[] for _ in range(4)]
    for l in range(depth):
        lam_init = 0.8 - 0.6 * math.exp(-0.3 * l)
        lam = (jnp.exp(jnp.sum(lambda_q1[l].astype(F32) * lambda_k1[l].astype(F32)))
               - jnp.exp(jnp.sum(lambda_q2[l].astype(F32) * lambda_k2[l].astype(F32)))
               + lam_init)
        scalars = jnp.concatenate([jnp.stack([lam, jnp.asarray(1.0 - lam_init, F32)]), far_bias]).astype(F32)
        gain = df_subln[l].astype(F32).reshape(1, 2 * HEAD_DIM)
        gains = norm_gains[l].astype(F32)

        x1, qkv, ksb, vsb, kdf, vdf = _pre_call(x, gains, ffa[0][l], ffa[1][l], ffa[2][l], w_in16[l])
        for rows, arr in zip(new_rows, (ksb, vsb, kdf, vdf)):
            rows.append(arr)

        osb_p = _sb_prompt_call(qkv, batch, seq, tq)
        odf_p = _df_prompt_call(qkv, scalars, bias_tiles, gain, batch, seq, tq)

        qkv_s = qkv[n_prompt:].reshape(dec_batch, dec_seq, N_GROUPS * GROUP_W)
        tail_rows = lambda arr: arr[n_prompt:].reshape(dec_batch, dec_seq, GROUP_W)
        new = lambda arr: _pad_axis(tail_rows(arr), 1, SUBLANES)
        osb_s = _dec_call("sb", l, pt_flat, n_pages, _sb_dec_queries(qkv_s[..., :GROUP_W]),
                          _sb_dec_new_rows(tail_rows(ksb), page), _sb_dec_new_rows(tail_rows(vsb), page),
                          caches[0], caches[1])
        odf_s = _dec_call("df", l, pt_flat, n_pages, _df_dec_queries(qkv_s[..., 3 * GROUP_W:4 * GROUP_W]),
                          new(kdf), new(vdf), caches[2], caches[3], (scalars, bias_dec, gain))

        tail = lambda o: o[:, :dec_seq].reshape(n_sample, GROUP_W).astype(BF16)
        osb = jnp.concatenate([osb_p, tail(osb_s)])
        odf = jnp.concatenate([odf_p, tail(odf_s)])
        x = _post_call(x1, osb, odf, gains, w_out16[l], ffb[0][l], ffb[1][l], ffb[2][l])

    y_prompt = x[:n_prompt].reshape(batch, seq, d_model)
    y_sample = x[n_prompt:].reshape(dec_batch, dec_seq, d_model)
    shapes = ((SB_HEADS, HEAD_DIM), (SB_HEADS, HEAD_DIM), (DF_HEADS, 2 * HEAD_DIM), (DF_HEADS, 2 * HEAD_DIM))
    prompt_rows = [jnp.stack([a[:n_prompt].reshape(batch, seq, *s) for a in rows])
                   for rows, s in zip(new_rows, shapes)]
    sample_rows = [jnp.stack([a[n_prompt:].reshape(dec_batch, dec_seq, *s) for a in rows])
                   for rows, s in zip(new_rows, shapes)]
    return (y_prompt, y_sample, *prompt_rows, *sample_rows)
```

```python
import functools
import math

import jax
import jax.numpy as jnp
from jax import lax
from jax.experimental import pallas as pl
from jax.experimental.pallas import tpu as pltpu

F32 = jnp.float32
BF16 = jnp.bfloat16

HEAD_DIM = 64
SB_HEADS = 8
DF_HEADS = 4
GROUP_W = SB_HEADS * HEAD_DIM
N_GROUPS = 6
NUM_BUCKETS = 32
MAX_EXACT = NUM_BUCKETS // 2
MAX_DISTANCE = 128
RMS_EPS = 1e-6
NEG_INF = -1e30
QK_SCALE = HEAD_DIM ** -0.5
LOG2E = 1.4426950408889634

LANES = 128
SUBLANES = 8
TOKEN_TILE = 512
ATTN_TILE = 256
ATTN_BLOCK_W = 256
SB_PHASE_GROUP = 2
FF_CHUNK = 256
VMEM_LIMIT_BYTES = 56 * 1024 * 1024


def _dot(a, b):
    return jnp.dot(a, b, preferred_element_type=F32)


def _dot_nt(a, b):
    return lax.dot_general(a, b, (((1,), (1,)), ((), ())), preferred_element_type=F32)


def _rms(x, g):
    ms = jnp.mean(x * x, axis=-1, keepdims=True)
    return x * lax.rsqrt(ms + RMS_EPS) * g


def _log2_sigmoid(z2):
    neg_abs = pltpu.bitcast(pltpu.bitcast(z2, jnp.uint32) | jnp.uint32(0x80000000), F32)
    return jnp.minimum(z2, 0.0) - jnp.log2(1.0 + jnp.exp2(neg_abs))


def _later_matrix(n):
    r = lax.broadcasted_iota(jnp.int32, (n, n), 0)
    c = lax.broadcasted_iota(jnp.int32, (n, n), 1)
    return (r > c).astype(BF16)


def _swiglu_residual(x, g_pre, g_post, wg_ref, wu_ref, wd_ref):
    xn = _rms(x, g_pre).astype(BF16)
    d_ff = wg_ref.shape[1]
    acc = None
    for c0 in range(0, d_ff, FF_CHUNK):
        c1 = min(c0 + FF_CHUNK, d_ff)
        gate = _dot(xn, wg_ref[:, c0:c1])
        up = _dot(xn, wu_ref[:, c0:c1])
        h = (gate * jax.nn.sigmoid(gate) * up).astype(BF16)
        y = _dot(h, wd_ref[c0:c1, :])
        acc = y if acc is None else acc + y
    return x + 0.5 * _rms(acc, g_post)


def _pre_kernel(x_ref, g_ref, wg_ref, wu_ref, wd_ref, win_ref,
                x1_ref, qkv_ref, ksb_ref, vsb_ref, kdf_ref, vdf_ref):
    g = g_ref[...]
    x1 = _swiglu_residual(x_ref[...], g[0:1], g[1:2], wg_ref, wu_ref, wd_ref)
    x1_ref[...] = x1
    h = _rms(x1, g[2:3]).astype(BF16)
    f32_out = (None, ksb_ref, vsb_ref, None, kdf_ref, vdf_ref)
    for c in range(N_GROUPS):
        lo, hi = c * GROUP_W, (c + 1) * GROUP_W
        p = _dot(h, win_ref[:, lo:hi])
        if f32_out[c] is None:
            p = p * QK_SCALE
        else:
            f32_out[c][...] = p
        qkv_ref[:, lo:hi] = p.astype(BF16)


def _post_kernel(x_ref, osb_ref, odf_ref, g_ref, wout_ref, wg_ref, wu_ref, wd_ref, y_ref):
    g = g_ref[...]
    mix = _dot(osb_ref[...], wout_ref[0:GROUP_W, :]) + _dot(odf_ref[...], wout_ref[GROUP_W:, :])
    x2 = x_ref[...] + _rms(mix, g[3:4])
    y_ref[...] = _swiglu_residual(x2, g[4:5], g[5:6], wg_ref, wu_ref, wd_ref)


def _token_tile(n):
    t = TOKEN_TILE
    while n % t:
        t //= 2
    return t


def _resident(stacked, layer):
    return pl.BlockSpec((None,) + stacked.shape[1:], lambda i: (layer, 0, 0), pipeline_mode=pl.Buffered(1))


def _rows(tm, width):
    return pl.BlockSpec((tm, width), lambda i: (i, 0))


def _params(n_axes):
    return pltpu.CompilerParams(dimension_semantics=("parallel",) * n_axes, vmem_limit_bytes=VMEM_LIMIT_BYTES)


def _pre_call(layer, x, gains, wg, wu, wd, win):
    n, d = x.shape
    tm = _token_tile(n)
    kv = jax.ShapeDtypeStruct((n, GROUP_W), F32)
    return pl.pallas_call(
        _pre_kernel,
        grid=(n // tm,),
        in_specs=[_rows(tm, d)] + [_resident(w, layer) for w in (gains, wg, wu, wd, win)],
        out_specs=[_rows(tm, d), _rows(tm, N_GROUPS * GROUP_W)] + [_rows(tm, GROUP_W)] * 4,
        out_shape=[jax.ShapeDtypeStruct((n, d), F32),
                   jax.ShapeDtypeStruct((n, N_GROUPS * GROUP_W), BF16), kv, kv, kv, kv],
        compiler_params=_params(1),
        name="pre",
    )(x, gains, wg, wu, wd, win)


def _post_call(layer, x, osb, odf, gains, wout, wg, wu, wd):
    n, d = x.shape
    tm = _token_tile(n)
    return pl.pallas_call(
        _post_kernel,
        grid=(n // tm,),
        in_specs=([_rows(tm, d), _rows(tm, GROUP_W), _rows(tm, GROUP_W)]
                  + [_resident(w, layer) for w in (gains, wout, wg, wu, wd)]),
        out_specs=_rows(tm, d),
        out_shape=jax.ShapeDtypeStruct((n, d), F32),
        compiler_params=_params(1),
        name="post",
    )(x, osb, odf, gains, wout, wg, wu, wd)


def _sb_prompt_kernel(q_ref, k_ref, v_ref, o_ref):
    i = pl.program_id(2)
    tq, width = q_ref.shape
    heads = width // HEAD_DIM
    later = _later_matrix(tq)
    r = lax.broadcasted_iota(jnp.int32, (tq, tq), 0)
    c = lax.broadcasted_iota(jnp.int32, (tq, tq), 1)
    diag_mask = c < r
    lanes = [slice(h * HEAD_DIM, (h + 1) * HEAD_DIM) for h in range(heads)]
    qs = [q_ref[:, sl] for sl in lanes]
    groups = [list(range(g, g + SB_PHASE_GROUP)) for g in range(0, heads, SB_PHASE_GROUP)]

    def scores(rows, hs):
        return [_dot_nt(qs[h], k_ref[rows, lanes[h]]) for h in hs]

    def log_terms(zs, mask):
        out = []
        for z in zs:
            z2 = z * LOG2E
            ls = _log2_sigmoid(z2)
            lstay = ls - z2
            if mask is not None:
                lstay = jnp.where(mask, lstay, 0.0)
            hi = lstay.astype(BF16)
            out.append((ls, lstay, hi, (lstay - hi.astype(F32)).astype(BF16)))
        return out

    def cumsums(terms):
        n = len(terms)
        local = _dot(jnp.concatenate([t[2] for t in terms] + [t[3] for t in terms], axis=0), later)
        return [local[k * tq:(k + 1) * tq] + local[(n + k) * tq:(n + k + 1) * tq] for k in range(n)]

    def weights(terms, local, runs, mask):
        ws, new_runs = [], []
        for (ls, lstay, _, _), loc, run in zip(terms, local, runs):
            w = jnp.exp2(ls + loc + run)
            if mask is not None:
                w = jnp.where(mask, w, 0.0)
            ws.append(w.astype(BF16))
            new_runs.append(run + jnp.sum(lstay, axis=-1, keepdims=True))
        return ws, new_runs

    def values(rows, hs, ws, accs):
        return [acc + _dot(w, v_ref[rows, lanes[h]]) for h, w, acc in zip(hs, ws, accs)]

    def tile(j, state, mask):
        rows = pl.ds(pl.multiple_of(j * tq, tq), tq)
        n = len(groups)
        accs = [[state[h][0] for h in hs] for hs in groups]
        runs = [[state[h][1] for h in hs] for hs in groups]
        z, terms, local, ws = [None] * n, [None] * n, [None] * n, [None] * n
        for step in range(n + 4):
            for g, hs in enumerate(groups):
                phase = step - g
                if phase == 0:
                    z[g] = scores(rows, hs)
                elif phase == 1:
                    terms[g] = log_terms(z[g], mask)
                elif phase == 2:
                    local[g] = cumsums(terms[g])
                elif phase == 3:
                    ws[g], runs[g] = weights(terms[g], local[g], runs[g], mask)
                elif phase == 4:
                    accs[g] = values(rows, hs, ws[g], accs[g])
        out = [None] * heads
        for g, hs in enumerate(groups):
            for k, h in enumerate(hs):
                out[h] = (accs[g][k], runs[g][k])
        return tuple(out)

    state = tuple((jnp.zeros((tq, HEAD_DIM), F32), jnp.zeros((tq, 1), F32)) for _ in range(heads))
    state = tile(i, state, diag_mask)
    state = lax.fori_loop(0, i, lambda t, s: tile(i - 1 - t, s, None), state)
    for h in range(heads):
        o_ref[:, lanes[h]] = state[h][0].astype(o_ref.dtype)


def _df_prompt_kernel(sc_ref, q_ref, k_ref, v_ref, bias_ref, gain_ref, o_ref):
    block = pl.program_id(1)
    i = pl.program_id(2)
    tq, width = q_ref.shape
    heads = width // LANES
    lam = sc_ref[0]
    post_scale = sc_ref[1]
    r = lax.broadcasted_iota(jnp.int32, (tq, tq), 0)
    c = lax.broadcasted_iota(jnp.int32, (tq, tq), 1)
    maps = [(h, m) for h in range(heads) for m in range(2)]
    lane = lambda h, m: slice(h * LANES + m * HEAD_DIM, h * LANES + (m + 1) * HEAD_DIM)
    qs = {hm: q_ref[:, lane(*hm)] for hm in maps}

    def scores(rows, h, bias_of, mask):
        out = []
        for m in range(2):
            s = _dot_nt(qs[(h, m)], k_ref[rows, lane(h, m)]) + bias_of(h)
            out.append(s if mask is None else jnp.where(mask, s, NEG_INF))
        return out

    def softmax_terms(ss, old):
        out = []
        for s, (mx, l, _) in zip(ss, old):
            m_new = jnp.maximum(mx, jnp.max(s, axis=-1, keepdims=True))
            alpha = jnp.exp(mx - m_new)
            p = jnp.exp(s - m_new)
            out.append((m_new, alpha * l + jnp.sum(p, axis=-1, keepdims=True), alpha, p.astype(BF16)))
        return out

    def values(rows, h, terms, old):
        vt = v_ref[rows, h * LANES:(h + 1) * LANES]
        return [(m_new, l, alpha * acc + _dot(p, vt)) for (m_new, l, alpha, p), (_, _, acc) in zip(terms, old)]

    def tile(j, state, bias_of, mask):
        rows = pl.ds(pl.multiple_of(j * tq, tq), tq)
        old = [[state[(h, m)] for m in range(2)] for h in range(heads)]
        ss, terms, new = [None] * heads, [None] * heads, [None] * heads
        for step in range(heads + 2):
            for h in range(heads):
                phase = step - h
                if phase == 0:
                    ss[h] = scores(rows, h, bias_of, mask)
                elif phase == 1:
                    terms[h] = softmax_terms(ss[h], old[h])
                elif phase == 2:
                    new[h] = values(rows, h, terms[h], old[h])
        return {(h, m): new[h][m] for h, m in maps}

    init = (jnp.full((tq, 1), NEG_INF, F32), jnp.zeros((tq, 1), F32), jnp.zeros((tq, LANES), F32))
    state = tile(i, {hm: init for hm in maps}, lambda h: bias_ref[h, 0], c <= r)
    has_prev = jnp.broadcast_to(i >= 1, (tq, tq))
    state = tile(jnp.maximum(i - 1, 0), state, lambda h: bias_ref[h, 1], has_prev)
    state = lax.fori_loop(0, jnp.maximum(i - 1, 0),
                          lambda j, s: tile(j, s, lambda h: sc_ref[2 + block * heads + h], None), state)
    for h in range(heads):
        (_, l1, a1), (_, l2, a2) = state[(h, 0)], state[(h, 1)]
        o = a1 / l1 - lam * (a2 / l2)
        y = o * lax.rsqrt(jnp.mean(o * o, axis=-1, keepdims=True) + RMS_EPS) * gain_ref[...]
        o_ref[:, h * LANES:(h + 1) * LANES] = (y * post_scale).astype(o_ref.dtype)


def _prompt_attention_call(body, name, qkv, group, batch, seq, tq, extra_operands=(), extra_specs=()):
    nq = seq // tq
    blocks = GROUP_W // ATTN_BLOCK_W
    col = lambda g: (lambda b, p, i: (b, g * blocks + p))
    q_spec = pl.BlockSpec((tq, ATTN_BLOCK_W), lambda b, p, i: (b * nq + i, group * blocks + p))
    kv_spec = lambda g: pl.BlockSpec((seq, ATTN_BLOCK_W), col(g))
    return pl.pallas_call(
        body,
        grid=(batch, blocks, nq),
        in_specs=list(extra_specs[:1]) + [q_spec, kv_spec(group + 1), kv_spec(group + 2)] + list(extra_specs[1:]),
        out_specs=pl.BlockSpec((tq, ATTN_BLOCK_W), lambda b, p, i: (b * nq + i, p)),
        out_shape=jax.ShapeDtypeStruct((batch * seq, GROUP_W), BF16),
        compiler_params=_params(3),
        name=name,
    )(*extra_operands[:1], qkv, qkv, qkv, *extra_operands[1:])


def _sb_prompt_call(qkv, batch, seq, tq):
    return _prompt_attention_call(_sb_prompt_kernel, "sb_prompt", qkv, 0, batch, seq, tq)


def _df_prompt_call(qkv, scalars, bias_tiles, gain, batch, seq, tq):
    heads = ATTN_BLOCK_W // LANES
    specs = [pl.BlockSpec(memory_space=pltpu.SMEM),
             pl.BlockSpec((heads, 2, tq, tq), lambda b, p, i: (p, 0, 0, 0)),
             pl.BlockSpec((1, LANES), lambda b, p, i: (0, 0))]
    return _prompt_attention_call(_df_prompt_kernel, "df_prompt", qkv, 3, batch, seq, tq,
                                  (scalars, bias_tiles, gain), specs)


def _new_page(new_ref, lanes, page):
    rows = new_ref[0][:, lanes]
    pad = jnp.zeros((page - rows.shape[0], rows.shape[1]), F32)
    return jnp.concatenate([rows, pad], axis=0).astype(BF16)


def _head_page(refs, new_ref, p, h, heads, page, lanes):
    if p == len(refs):
        return _new_page(new_ref, lanes, page)
    return refs[p][0, 0, pl.ds(h, page, stride=heads), :].astype(BF16)


def _sb_dec_kernel(n_pages, pt_ref, q_ref, knew_ref, vnew_ref, *refs):
    del pt_ref
    k_refs, v_refs = refs[:n_pages], refs[n_pages:2 * n_pages]
    o_ref, s_ref = refs[2 * n_pages:]
    rows, width = q_ref.shape[1:]
    page = k_refs[0].shape[3]
    blocks = n_pages + 1
    qrows = rows // SB_HEADS
    key_block = lambda p: slice(p * page, (p + 1) * page)
    page_of = lambda pages, new_ref, p: (new_ref[0] if p == n_pages else pages[p][0, 0]).astype(BF16)
    q = q_ref[0].astype(BF16)
    for p in range(blocks):
        s_ref[:, key_block(p)] = _dot(q, page_of(k_refs, knew_ref, p))
    z = s_ref[...]
    lane = lax.broadcasted_iota(jnp.int32, z.shape, 1)
    query = lax.broadcasted_iota(jnp.int32, z.shape, 0) % qrows
    mask = lane < n_pages * page + query
    z2 = z * LOG2E
    ls = _log2_sigmoid(z2)
    lstay = jnp.where(mask, ls - z2, 0.0)
    hi = lstay.astype(BF16)
    lo = (lstay - hi.astype(F32)).astype(BF16)
    stacked = jnp.concatenate([part[:, key_block(p)] for part in (hi, lo) for p in range(blocks)], axis=0)
    local = _dot(stacked, _later_matrix(page))
    run = jnp.zeros((rows, 1), F32)
    acc = jnp.zeros((rows, width), F32)
    for p in range(blocks - 1, -1, -1):
        after = local[p * rows:(p + 1) * rows] + local[(blocks + p) * rows:(blocks + p + 1) * rows] + run
        w = jnp.where(mask[:, key_block(p)], jnp.exp2(ls[:, key_block(p)] + after), 0.0)
        acc = acc + _dot_nt(w.astype(BF16), page_of(v_refs, vnew_ref, p))
        run = run + jnp.sum(lstay[:, key_block(p)], axis=-1, keepdims=True)
    for h in range(SB_HEADS):
        lanes = slice(h * HEAD_DIM, (h + 1) * HEAD_DIM)
        o_ref[0, :, lanes] = acc[h * qrows:(h + 1) * qrows, lanes]


def _df_dec_kernel(n_pages, pt_ref, sc_ref, q_ref, knew_ref, vnew_ref, bias_ref, gain_ref, *refs):
    del pt_ref
    k_refs, v_refs = refs[:n_pages], refs[n_pages:2 * n_pages]
    o_ref, s_ref = refs[2 * n_pages:]
    heads, hrows, width = q_ref.shape[1:]
    qrows = hrows // 2
    page = k_refs[0].shape[2] // heads
    blocks = n_pages + 1
    lam = sc_ref[0]
    post_scale = sc_ref[1]
    key_block = lambda p: slice(p * page, (p + 1) * page)
    head_rows = lambda h: slice(h * hrows, (h + 1) * hrows)
    head_lanes = lambda h: slice(h * width, (h + 1) * width)
    for h in range(heads):
        qh = q_ref[0, h].astype(BF16)
        for p in range(blocks):
            s_ref[head_rows(h), key_block(p)] = (
                _dot_nt(qh, _head_page(k_refs, knew_ref, p, h, heads, page, head_lanes(h)))
                + bias_ref[head_rows(h), key_block(p)])
    rows = heads * hrows
    new_mask = (lax.broadcasted_iota(jnp.int32, (rows, page), 1)
                <= lax.broadcasted_iota(jnp.int32, (rows, page), 0) % qrows)
    s_ref[:, key_block(n_pages)] = jnp.where(new_mask, s_ref[:, key_block(n_pages)], NEG_INF)
    s = s_ref[...]
    e = jnp.exp(s - jnp.max(s, axis=-1, keepdims=True))
    second_map = lax.broadcasted_iota(jnp.int32, (rows, 1), 0) % hrows >= qrows
    coef = jnp.where(second_map, -lam, 1.0) / jnp.sum(e, axis=-1, keepdims=True)
    s_ref[...] = e * coef
    gain = gain_ref[...]
    for h in range(heads):
        acc = jnp.zeros((hrows, width), F32)
        for p in range(blocks):
            acc = acc + _dot(s_ref[head_rows(h), key_block(p)].astype(BF16),
                             _head_page(v_refs, vnew_ref, p, h, heads, page, head_lanes(h)))
        o = acc[:qrows] + acc[qrows:]
        y = o * lax.rsqrt(jnp.mean(o * o, axis=-1, keepdims=True) + RMS_EPS) * gain
        o_ref[0, :, head_lanes(h)] = y * post_scale


def _dec_call(kind, layer, page_table_flat, n_pages, q, knew, vnew, cache_k, cache_v, extras=()):
    n_samples = q.shape[0]
    page_block = (1, 1) + cache_k.shape[2:]

    def page_spec(p):
        return pl.BlockSpec(page_block, lambda b, pt: (layer, pt[b * n_pages + p], 0, 0))

    per_sample = lambda arr: pl.BlockSpec((1,) + arr.shape[1:], lambda b, pt: (b,) + (0,) * (arr.ndim - 1))
    whole = lambda arr: pl.BlockSpec(arr.shape, lambda b, pt: (0,) * arr.ndim)
    pages = [page_spec(p) for p in range(n_pages)]
    common = [per_sample(q), per_sample(knew), per_sample(vnew)]
    if kind == "sb":
        body = functools.partial(_sb_dec_kernel, n_pages)
        in_specs, operands = common, [q, knew, vnew]
        scratch = [pltpu.VMEM((q.shape[1], (n_pages + 1) * cache_k.shape[3]), F32)]
    else:
        scalars, bias, gain = extras
        body = functools.partial(_df_dec_kernel, n_pages)
        in_specs = [pl.BlockSpec(memory_space=pltpu.SMEM)] + common + [whole(bias), whole(gain)]
        operands = [scalars, q, knew, vnew, bias, gain]
        scratch = [pltpu.VMEM(bias.shape, F32)]
    return pl.pallas_call(
        body,
        grid_spec=pltpu.PrefetchScalarGridSpec(
            num_scalar_prefetch=1,
            grid=(n_samples,),
            in_specs=in_specs + pages + pages,
            out_specs=pl.BlockSpec((1, SUBLANES, GROUP_W), lambda b, pt: (b, 0, 0)),
            scratch_shapes=scratch),
        out_shape=jax.ShapeDtypeStruct((n_samples, SUBLANES, GROUP_W), F32),
        compiler_params=_params(1),
        name=kind + "_dec",
    )(page_table_flat, *operands, *([cache_k] * n_pages), *([cache_v] * n_pages))


def _rel_bucket(dist):
    n = jnp.maximum(dist, 0)
    nf = jnp.maximum(n, 1).astype(F32)
    large = MAX_EXACT + (jnp.log(nf / MAX_EXACT) / math.log(MAX_DISTANCE / MAX_EXACT)
                         * (NUM_BUCKETS - MAX_EXACT)).astype(jnp.int32)
    large = jnp.minimum(large, NUM_BUCKETS - 1)
    return jnp.where(n < MAX_EXACT, n, large)


def _bias_of_dist(rel_bias, dist):
    return rel_bias.astype(F32)[_rel_bucket(dist)].T


def _toeplitz(v, n):
    heads, period = v.shape
    return jnp.tile(v, (1, n))[:, :n * (period - 1)].reshape(heads, n, period - 1)[:, :, :n]


def _prompt_bias_tiles(rel_bias, tq):
    m = jnp.arange(2 * tq, dtype=jnp.int32)
    diag = jnp.where(m < tq, -m, 2 * tq - m)
    prev = jnp.where(m < tq, tq - m, 3 * tq - m)
    return jnp.stack([_toeplitz(_bias_of_dist(rel_bias, diag), tq),
                      _toeplitz(_bias_of_dist(rel_bias, prev), tq)], axis=1)


def _dec_bias(rel_bias, past_len, n_keys, qrows):
    u = _bias_of_dist(rel_bias, past_len + (qrows - 1) - jnp.arange(n_keys + qrows - 1, dtype=jnp.int32))
    per_q = jnp.stack([u[:, qrows - 1 - q:qrows - 1 - q + n_keys] for q in range(qrows)], axis=1)
    heads = per_q.shape[0]
    return jnp.broadcast_to(per_q[:, None], (heads, 2, qrows, n_keys)).reshape(heads * 2 * qrows, n_keys)


def _pad_axis(x, axis, size):
    pad = [(0, 0)] * x.ndim
    pad[axis] = (0, size - x.shape[axis])
    return jnp.pad(x, pad)


def _sb_dec_queries(q):
    n, t, w = q.shape
    qh = _pad_axis(q.astype(F32).reshape(n, 1, t, SB_HEADS, HEAD_DIM), 2, SUBLANES)
    keep = jnp.eye(SB_HEADS, dtype=bool)[None, :, None, :, None]
    return jnp.where(keep, qh, 0.0).reshape(n, SB_HEADS * SUBLANES, w)


def _sb_dec_new_rows(rows, page):
    return _pad_axis(rows.transpose(0, 2, 1), 2, page)


def _df_dec_queries(q):
    n, t, _ = q.shape
    qh = q.astype(F32).reshape(n, t, DF_HEADS, 2, HEAD_DIM).transpose(0, 2, 3, 1, 4)
    qh = _pad_axis(qh, 3, SUBLANES)[:, :, :, :, None, :]
    keep = jnp.eye(2, dtype=bool)[None, None, :, None, :, None]
    return jnp.where(keep, qh, 0.0).reshape(n, DF_HEADS, 2 * SUBLANES, 2 * HEAD_DIM)


def kernel(x_prompt, x_sample, cache_sb_k, cache_sb_v, cache_df_k, cache_df_v, page_table, norm_gains, w_in, w_out, ffa_gate, ffa_up, ffa_down, ffb_gate, ffb_up, ffb_down, df_subln, lambda_q1, lambda_k1, lambda_q2, lambda_k2, rel_bias):
    batch, seq, d_model = x_prompt.shape
    dec_batch, dec_seq, _ = x_sample.shape
    depth, n_phys, page = cache_sb_k.shape[:3]
    n_pages = page_table.shape[1]
    past_len = n_pages * page
    n_prompt, n_sample = batch * seq, dec_batch * dec_seq
    tq = min(ATTN_TILE, seq)
    assert dec_seq <= SUBLANES and seq % tq == 0 and tq > MAX_DISTANCE

    x = jnp.concatenate([x_prompt.reshape(n_prompt, d_model), x_sample.reshape(n_sample, d_model)])
    caches = [jnp.transpose(c, (0, 1, 3, 4, 2)).reshape(depth, n_phys, GROUP_W, page)
              for c in (cache_sb_k, cache_sb_v)]
    caches += [c.reshape(depth, n_phys, page * DF_HEADS, 2 * HEAD_DIM) for c in (cache_df_k, cache_df_v)]
    pt_flat = page_table.reshape(-1).astype(jnp.int32)
    to16 = lambda w: w.astype(BF16)
    w_in16, w_out16 = to16(w_in), to16(w_out)
    ffa = [to16(w) for w in (ffa_gate, ffa_up, ffa_down)]
    ffb = [to16(w) for w in (ffb_gate, ffb_up, ffb_down)]
    gains = norm_gains.astype(F32)

    bias_tiles = _prompt_bias_tiles(rel_bias, tq)
    bias_dec = _dec_bias(rel_bias, past_len, past_len + page, SUBLANES)
    far_bias = rel_bias.astype(F32)[NUM_BUCKETS - 1]

    new_rows = [[] for _ in range(4)]
    for l in range(depth):
        lam_init = 0.8 - 0.6 * math.exp(-0.3 * l)
        lam = (jnp.exp(jnp.sum(lambda_q1[l].astype(F32) * lambda_k1[l].astype(F32)))
               - jnp.exp(jnp.sum(lambda_q2[l].astype(F32) * lambda_k2[l].astype(F32)))
               + lam_init)
        scalars = jnp.concatenate([jnp.stack([lam, jnp.asarray(1.0 - lam_init, F32)]), far_bias]).astype(F32)
        gain = df_subln[l].astype(F32).reshape(1, 2 * HEAD_DIM)

        x1, qkv, ksb, vsb, kdf, vdf = _pre_call(l, x, gains, *ffa, w_in16)
        for rows, arr in zip(new_rows, (ksb, vsb, kdf, vdf)):
            rows.append(arr)

        osb_p = _sb_prompt_call(qkv, batch, seq, tq)
        odf_p = _df_prompt_call(qkv, scalars, bias_tiles, gain, batch, seq, tq)

        qkv_s = qkv[n_prompt:].reshape(dec_batch, dec_seq, N_GROUPS * GROUP_W)
        tail_rows = lambda arr: arr[n_prompt:].reshape(dec_batch, dec_seq, GROUP_W)
        new = lambda arr: _pad_axis(tail_rows(arr), 1, SUBLANES)
        osb_s = _dec_call("sb", l, pt_flat, n_pages, _sb_dec_queries(qkv_s[..., :GROUP_W]),
                          _sb_dec_new_rows(tail_rows(ksb), page), _sb_dec_new_rows(tail_rows(vsb), page),
                          caches[0], caches[1])
        odf_s = _dec_call("df", l, pt_flat, n_pages, _df_dec_queries(qkv_s[..., 3 * GROUP_W:4 * GROUP_W]),
                          new(kdf), new(vdf), caches[2], caches[3], (scalars, bias_dec, gain))

        tail = lambda o: o[:, :dec_seq].reshape(n_sample, GROUP_W).astype(BF16)
        osb = jnp.concatenate([osb_p, tail(osb_s)])
        odf = jnp.concatenate([odf_p, tail(odf_s)])
        x = _post_call(l, x1, osb, odf, gains, w_out16, *ffb)

    y_prompt = x[:n_prompt].reshape(batch, seq, d_model)
    y_sample = x[n_prompt:].reshape(dec_batch, dec_seq, d_model)
    shapes = ((SB_HEADS, HEAD_DIM), (SB_HEADS, HEAD_DIM), (DF_HEADS, 2 * HEAD_DIM), (DF_HEADS, 2 * HEAD_DIM))
    prompt_rows = [jnp.stack([a[:n_prompt].reshape(batch, seq, *s) for a in rows])
                   for rows, s in zip(new_rows, shapes)]
    sample_rows = [jnp.stack([a[n_prompt:].reshape(dec_batch, dec_seq, *s) for a in rows])
                   for rows, s in zip(new_rows, shapes)]
    return (y_prompt, y_sample, *prompt_rows, *sample_rows)
```

```python
import functools
import math

import jax
import jax.numpy as jnp
from jax import lax
from jax.experimental import pallas as pl
from jax.experimental.pallas import tpu as pltpu

F32 = jnp.float32
BF16 = jnp.bfloat16

HEAD_DIM = 64
SB_HEADS = 8
DF_HEADS = 4
GROUP_W = SB_HEADS * HEAD_DIM
N_GROUPS = 6
NUM_BUCKETS = 32
MAX_EXACT = NUM_BUCKETS // 2
MAX_DISTANCE = 128
RMS_EPS = 1e-6
NEG_INF = -1e30
QK_SCALE = HEAD_DIM ** -0.5
LOG2E = 1.4426950408889634

LANES = 128
SUBLANES = 8
TOKEN_TILE = 512
ATTN_TILE = 256
ATTN_BLOCK_W = 256
SB_PHASE_GROUP = 2
FF_CHUNK = 256
VMEM_LIMIT_BYTES = 56 * 1024 * 1024


def _dot(a, b):
    return jnp.dot(a, b, preferred_element_type=F32)


def _dot_nt(a, b):
    return lax.dot_general(a, b, (((1,), (1,)), ((), ())), preferred_element_type=F32)


def _rms(x, g):
    ms = jnp.mean(x * x, axis=-1, keepdims=True)
    return x * lax.rsqrt(ms + RMS_EPS) * g


def _log2_sigmoid(z2):
    neg_abs = pltpu.bitcast(pltpu.bitcast(z2, jnp.uint32) | jnp.uint32(0x80000000), F32)
    return jnp.minimum(z2, 0.0) - jnp.log2(1.0 + jnp.exp2(neg_abs))


def _later_matrix(n):
    r = lax.broadcasted_iota(jnp.int32, (n, n), 0)
    c = lax.broadcasted_iota(jnp.int32, (n, n), 1)
    return (r > c).astype(BF16)


def _swiglu_residual(x, g_pre, g_post, wg_ref, wu_ref, wd_ref):
    xn = _rms(x, g_pre).astype(BF16)
    d_ff = wg_ref.shape[1]
    acc = None
    for c0 in range(0, d_ff, FF_CHUNK):
        c1 = min(c0 + FF_CHUNK, d_ff)
        gate = _dot(xn, wg_ref[:, c0:c1])
        up = _dot(xn, wu_ref[:, c0:c1])
        h = (gate * jax.nn.sigmoid(gate) * up).astype(BF16)
        y = _dot(h, wd_ref[c0:c1, :])
        acc = y if acc is None else acc + y
    return x + 0.5 * _rms(acc, g_post)


def _pre_kernel(prompt_tiles, n_carried, x_ref, g_ref, wg_ref, wu_ref, wd_ref, win_ref, *refs):
    x1_ref, qkv_ref = refs[n_carried:n_carried + 2]
    kv_refs = refs[n_carried + 2:]
    is_prompt = pl.program_id(0) < prompt_tiles
    g = g_ref[...]
    x1 = _swiglu_residual(x_ref[...], g[0:1], g[1:2], wg_ref, wu_ref, wd_ref)
    x1_ref[...] = x1
    h = _rms(x1, g[2:3]).astype(BF16)
    kv_pair = {1: 0, 2: 1, 4: 2, 5: 3}
    for c in range(N_GROUPS):
        lo, hi = c * GROUP_W, (c + 1) * GROUP_W
        p = _dot(h, win_ref[:, lo:hi])
        if c in kv_pair:
            prompt_ref, sample_ref = kv_refs[2 * kv_pair[c]:2 * kv_pair[c] + 2]

            @pl.when(is_prompt)
            def _(p=p, ref=prompt_ref):
                ref[...] = p

            @pl.when(jnp.logical_not(is_prompt))
            def _(p=p, ref=sample_ref):
                ref[...] = p
        else:
            p = p * QK_SCALE
        qkv_ref[:, lo:hi] = p.astype(BF16)


def _post_kernel(prompt_tiles, x_ref, osb_p_ref, osb_s_ref, odf_p_ref, odf_s_ref,
                 g_ref, wout_ref, wg_ref, wu_ref, wd_ref, y_ref):
    is_prompt = pl.program_id(0) < prompt_tiles
    osb = jnp.where(is_prompt, osb_p_ref[...], osb_s_ref[...])
    odf = jnp.where(is_prompt, odf_p_ref[...], odf_s_ref[...])
    g = g_ref[...]
    mix = _dot(osb, wout_ref[0:GROUP_W, :]) + _dot(odf, wout_ref[GROUP_W:, :])
    x2 = x_ref[...] + _rms(mix, g[3:4])
    y_ref[...] = _swiglu_residual(x2, g[4:5], g[5:6], wg_ref, wu_ref, wd_ref)


def _token_tile(n):
    t = TOKEN_TILE
    while n % t:
        t //= 2
    return t


def _resident(stacked, layer):
    return pl.BlockSpec((None,) + stacked.shape[1:], lambda i: (layer, 0, 0), pipeline_mode=pl.Buffered(1))


def _rows(tm, width):
    return pl.BlockSpec((tm, width), lambda i: (i, 0))


def _params(n_axes):
    return pltpu.CompilerParams(dimension_semantics=("parallel",) * n_axes, vmem_limit_bytes=VMEM_LIMIT_BYTES)


def _prompt_rows(tm, width, prompt_tiles, layer=None):
    index = lambda i: jnp.minimum(i, prompt_tiles - 1)
    if layer is None:
        return pl.BlockSpec((tm, width), lambda i: (index(i), 0))
    return pl.BlockSpec((None, tm, width), lambda i: (layer, index(i), 0))


def _sample_rows(tm, width, prompt_tiles, layer=None):
    index = lambda i: jnp.maximum(i - prompt_tiles, 0)
    if layer is None:
        return pl.BlockSpec((tm, width), lambda i: (index(i), 0))
    return pl.BlockSpec((None, tm, width), lambda i: (layer, index(i), 0))


def _sequential():
    return pltpu.CompilerParams(dimension_semantics=("arbitrary",), vmem_limit_bytes=VMEM_LIMIT_BYTES)


def _pre_call(layer, x, n_prompt, gains, wg, wu, wd, win, carried):
    n, d = x.shape
    depth = gains.shape[0]
    tm = _token_tile(math.gcd(n_prompt, n - n_prompt))
    prompt_tiles = n_prompt // tm
    carried = list(carried or [])
    kv_specs, kv_shapes, stacked_at = [], [], []
    for kind in range(4):
        if kind < 2:
            kv_specs.append(_prompt_rows(tm, GROUP_W, prompt_tiles))
            kv_shapes.append(jax.ShapeDtypeStruct((n_prompt, GROUP_W), F32))
        else:
            stacked_at.append(len(kv_specs))
            kv_specs.append(_prompt_rows(tm, GROUP_W, prompt_tiles, layer))
            kv_shapes.append(jax.ShapeDtypeStruct((depth, n_prompt, GROUP_W), F32))
        stacked_at.append(len(kv_specs))
        kv_specs.append(_sample_rows(tm, GROUP_W, prompt_tiles, layer))
        kv_shapes.append(jax.ShapeDtypeStruct((depth, n - n_prompt, GROUP_W), F32))
    n_fixed_in, n_fixed_out = 6, 2
    outs = pl.pallas_call(
        functools.partial(_pre_kernel, prompt_tiles, len(carried)),
        grid=(n // tm,),
        in_specs=([_rows(tm, d)] + [_resident(w, layer) for w in (gains, wg, wu, wd, win)]
                  + [pl.BlockSpec(memory_space=pl.ANY)] * len(carried)),
        out_specs=[_rows(tm, d), _rows(tm, N_GROUPS * GROUP_W)] + kv_specs,
        out_shape=[jax.ShapeDtypeStruct((n, d), F32),
                   jax.ShapeDtypeStruct((n, N_GROUPS * GROUP_W), BF16)] + kv_shapes,
        input_output_aliases={n_fixed_in + k: n_fixed_out + stacked_at[k] for k in range(len(carried))},
        compiler_params=_sequential(),
        name="pre",
    )(x, gains, wg, wu, wd, win, *carried)
    kv = outs[n_fixed_out:]
    return outs[0], outs[1], kv, [kv[j] for j in stacked_at]


def _post_call(layer, x, n_prompt, osb_p, osb_s, odf_p, odf_s, gains, wout, wg, wu, wd):
    n, d = x.shape
    tm = _token_tile(math.gcd(n_prompt, n - n_prompt))
    prompt_tiles = n_prompt // tm
    attn = [_prompt_rows(tm, GROUP_W, prompt_tiles), _sample_rows(tm, GROUP_W, prompt_tiles)]
    return pl.pallas_call(
        functools.partial(_post_kernel, prompt_tiles),
        grid=(n // tm,),
        in_specs=[_rows(tm, d)] + attn + attn + [_resident(w, layer) for w in (gains, wout, wg, wu, wd)],
        out_specs=_rows(tm, d),
        out_shape=jax.ShapeDtypeStruct((n, d), F32),
        compiler_params=_sequential(),
        name="post",
    )(x, osb_p, osb_s, odf_p, odf_s, gains, wout, wg, wu, wd)


def _sb_prompt_kernel(q_ref, k_ref, v_ref, o_ref):
    i = pl.program_id(2)
    tq, width = q_ref.shape
    heads = width // HEAD_DIM
    later = _later_matrix(tq)
    r = lax.broadcasted_iota(jnp.int32, (tq, tq), 0)
    c = lax.broadcasted_iota(jnp.int32, (tq, tq), 1)
    diag_mask = c < r
    lanes = [slice(h * HEAD_DIM, (h + 1) * HEAD_DIM) for h in range(heads)]
    qs = [q_ref[:, sl] for sl in lanes]
    groups = [list(range(g, g + SB_PHASE_GROUP)) for g in range(0, heads, SB_PHASE_GROUP)]

    def scores(rows, hs):
        return [_dot_nt(qs[h], k_ref[rows, lanes[h]]) for h in hs]

    def log_terms(zs, mask):
        out = []
        for z in zs:
            z2 = z * LOG2E
            ls = _log2_sigmoid(z2)
            lstay = ls - z2
            if mask is not None:
                lstay = jnp.where(mask, lstay, 0.0)
            hi = lstay.astype(BF16)
            out.append((ls, lstay, hi, (lstay - hi.astype(F32)).astype(BF16)))
        return out

    def cumsums(terms):
        n = len(terms)
        local = _dot(jnp.concatenate([t[2] for t in terms] + [t[3] for t in terms], axis=0), later)
        return [local[k * tq:(k + 1) * tq] + local[(n + k) * tq:(n + k + 1) * tq] for k in range(n)]

    def weights(terms, local, runs, mask):
        ws, new_runs = [], []
        for (ls, lstay, _, _), loc, run in zip(terms, local, runs):
            w = jnp.exp2(ls + loc + run)
            if mask is not None:
                w = jnp.where(mask, w, 0.0)
            ws.append(w.astype(BF16))
            new_runs.append(run + jnp.sum(lstay, axis=-1, keepdims=True))
        return ws, new_runs

    def values(rows, hs, ws, accs):
        return [acc + _dot(w, v_ref[rows, lanes[h]]) for h, w, acc in zip(hs, ws, accs)]

    def tile(j, state, mask):
        rows = pl.ds(pl.multiple_of(j * tq, tq), tq)
        n = len(groups)
        accs = [[state[h][0] for h in hs] for hs in groups]
        runs = [[state[h][1] for h in hs] for hs in groups]
        z, terms, local, ws = [None] * n, [None] * n, [None] * n, [None] * n
        for step in range(n + 4):
            for g, hs in enumerate(groups):
                phase = step - g
                if phase == 0:
                    z[g] = scores(rows, hs)
                elif phase == 1:
                    terms[g] = log_terms(z[g], mask)
                elif phase == 2:
                    local[g] = cumsums(terms[g])
                elif phase == 3:
                    ws[g], runs[g] = weights(terms[g], local[g], runs[g], mask)
                elif phase == 4:
                    accs[g] = values(rows, hs, ws[g], accs[g])
        out = [None] * heads
        for g, hs in enumerate(groups):
            for k, h in enumerate(hs):
                out[h] = (accs[g][k], runs[g][k])
        return tuple(out)

    state = tuple((jnp.zeros((tq, HEAD_DIM), F32), jnp.zeros((tq, 1), F32)) for _ in range(heads))
    state = tile(i, state, diag_mask)
    state = lax.fori_loop(0, i, lambda t, s: tile(i - 1 - t, s, None), state)
    for h in range(heads):
        o_ref[:, lanes[h]] = state[h][0].astype(o_ref.dtype)


def _df_prompt_kernel(sc_ref, q_ref, k_ref, v_ref, bias_ref, gain_ref, o_ref):
    block = pl.program_id(1)
    i = pl.program_id(2)
    tq, width = q_ref.shape
    heads = width // LANES
    lam = sc_ref[0]
    post_scale = sc_ref[1]
    r = lax.broadcasted_iota(jnp.int32, (tq, tq), 0)
    c = lax.broadcasted_iota(jnp.int32, (tq, tq), 1)
    maps = [(h, m) for h in range(heads) for m in range(2)]
    lane = lambda h, m: slice(h * LANES + m * HEAD_DIM, h * LANES + (m + 1) * HEAD_DIM)
    qs = {hm: q_ref[:, lane(*hm)] for hm in maps}

    def scores(rows, h, bias_of, mask):
        out = []
        for m in range(2):
            s = _dot_nt(qs[(h, m)], k_ref[rows, lane(h, m)]) + bias_of(h)
            out.append(s if mask is None else jnp.where(mask, s, NEG_INF))
        return out

    def softmax_terms(ss, old):
        out = []
        for s, (mx, l, _) in zip(ss, old):
            m_new = jnp.maximum(mx, jnp.max(s, axis=-1, keepdims=True))
            alpha = jnp.exp(mx - m_new)
            p = jnp.exp(s - m_new)
            out.append((m_new, alpha * l + jnp.sum(p, axis=-1, keepdims=True), alpha, p.astype(BF16)))
        return out

    def values(rows, h, terms, old):
        vt = v_ref[rows, h * LANES:(h + 1) * LANES]
        return [(m_new, l, alpha * acc + _dot(p, vt)) for (m_new, l, alpha, p), (_, _, acc) in zip(terms, old)]

    def tile(j, state, bias_of, mask):
        rows = pl.ds(pl.multiple_of(j * tq, tq), tq)
        old = [[state[(h, m)] for m in range(2)] for h in range(heads)]
        ss, terms, new = [None] * heads, [None] * heads, [None] * heads
        for step in range(heads + 2):
            for h in range(heads):
                phase = step - h
                if phase == 0:
                    ss[h] = scores(rows, h, bias_of, mask)
                elif phase == 1:
                    terms[h] = softmax_terms(ss[h], old[h])
                elif phase == 2:
                    new[h] = values(rows, h, terms[h], old[h])
        return {(h, m): new[h][m] for h, m in maps}

    init = (jnp.full((tq, 1), NEG_INF, F32), jnp.zeros((tq, 1), F32), jnp.zeros((tq, LANES), F32))
    state = tile(i, {hm: init for hm in maps}, lambda h: bias_ref[h, 0], c <= r)
    has_prev = jnp.broadcast_to(i >= 1, (tq, tq))
    state = tile(jnp.maximum(i - 1, 0), state, lambda h: bias_ref[h, 1], has_prev)
    state = lax.fori_loop(0, jnp.maximum(i - 1, 0),
                          lambda j, s: tile(j, s, lambda h: sc_ref[2 + block * heads + h], None), state)
    for h in range(heads):
        (_, l1, a1), (_, l2, a2) = state[(h, 0)], state[(h, 1)]
        o = a1 / l1 - lam * (a2 / l2)
        y = o * lax.rsqrt(jnp.mean(o * o, axis=-1, keepdims=True) + RMS_EPS) * gain_ref[...]
        o_ref[:, h * LANES:(h + 1) * LANES] = (y * post_scale).astype(o_ref.dtype)


def _prompt_attention_call(body, name, qkv, group, batch, seq, tq, extra_operands=(), extra_specs=()):
    nq = seq // tq
    blocks = GROUP_W // ATTN_BLOCK_W
    col = lambda g: (lambda b, p, i: (b, g * blocks + p))
    q_spec = pl.BlockSpec((tq, ATTN_BLOCK_W), lambda b, p, i: (b * nq + i, group * blocks + p))
    kv_spec = lambda g: pl.BlockSpec((seq, ATTN_BLOCK_W), col(g))
    return pl.pallas_call(
        body,
        grid=(batch, blocks, nq),
        in_specs=list(extra_specs[:1]) + [q_spec, kv_spec(group + 1), kv_spec(group + 2)] + list(extra_specs[1:]),
        out_specs=pl.BlockSpec((tq, ATTN_BLOCK_W), lambda b, p, i: (b * nq + i, p)),
        out_shape=jax.ShapeDtypeStruct((batch * seq, GROUP_W), BF16),
        compiler_params=_params(3),
        name=name,
    )(*extra_operands[:1], qkv, qkv, qkv, *extra_operands[1:])


def _sb_prompt_call(qkv, batch, seq, tq):
    return _prompt_attention_call(_sb_prompt_kernel, "sb_prompt", qkv, 0, batch, seq, tq)


def _df_prompt_call(qkv, scalars, bias_tiles, gain, batch, seq, tq):
    heads = ATTN_BLOCK_W // LANES
    specs = [pl.BlockSpec(memory_space=pltpu.SMEM),
             pl.BlockSpec((heads, 2, tq, tq), lambda b, p, i: (p, 0, 0, 0)),
             pl.BlockSpec((1, LANES), lambda b, p, i: (0, 0))]
    return _prompt_attention_call(_df_prompt_kernel, "df_prompt", qkv, 3, batch, seq, tq,
                                  (scalars, bias_tiles, gain), specs)


def _new_page(new_ref, lanes, page):
    rows = new_ref[0][:, lanes]
    pad = jnp.zeros((page - rows.shape[0], rows.shape[1]), F32)
    return jnp.concatenate([rows, pad], axis=0).astype(BF16)


def _head_page(refs, new_ref, p, h, heads, page, lanes):
    if p == len(refs):
        return _new_page(new_ref, lanes, page)
    return refs[p][0, 0, pl.ds(h, page, stride=heads), :].astype(BF16)


def _sb_dec_kernel(n_pages, pt_ref, q_ref, knew_ref, vnew_ref, *refs):
    del pt_ref
    k_refs, v_refs = refs[:n_pages], refs[n_pages:2 * n_pages]
    o_ref, s_ref = refs[2 * n_pages:]
    rows, width = q_ref.shape[1:]
    page = k_refs[0].shape[3]
    blocks = n_pages + 1
    qrows = rows // SB_HEADS
    key_block = lambda p: slice(p * page, (p + 1) * page)
    page_of = lambda pages, new_ref, p: (new_ref[0] if p == n_pages else pages[p][0, 0]).astype(BF16)
    q = q_ref[0].astype(BF16)
    for p in range(blocks):
        s_ref[:, key_block(p)] = _dot(q, page_of(k_refs, knew_ref, p))
    z = s_ref[...]
    lane = lax.broadcasted_iota(jnp.int32, z.shape, 1)
    query = lax.broadcasted_iota(jnp.int32, z.shape, 0) % qrows
    mask = lane < n_pages * page + query
    z2 = z * LOG2E
    ls = _log2_sigmoid(z2)
    lstay = jnp.where(mask, ls - z2, 0.0)
    hi = lstay.astype(BF16)
    lo = (lstay - hi.astype(F32)).astype(BF16)
    stacked = jnp.concatenate([part[:, key_block(p)] for part in (hi, lo) for p in range(blocks)], axis=0)
    local = _dot(stacked, _later_matrix(page))
    run = jnp.zeros((rows, 1), F32)
    acc = jnp.zeros((rows, width), F32)
    for p in range(blocks - 1, -1, -1):
        after = local[p * rows:(p + 1) * rows] + local[(blocks + p) * rows:(blocks + p + 1) * rows] + run
        w = jnp.where(mask[:, key_block(p)], jnp.exp2(ls[:, key_block(p)] + after), 0.0)
        acc = acc + _dot_nt(w.astype(BF16), page_of(v_refs, vnew_ref, p))
        run = run + jnp.sum(lstay[:, key_block(p)], axis=-1, keepdims=True)
    for h in range(SB_HEADS):
        lanes = slice(h * HEAD_DIM, (h + 1) * HEAD_DIM)
        o_ref[0, :, lanes] = acc[h * qrows:(h + 1) * qrows, lanes]


def _df_dec_kernel(n_pages, pt_ref, sc_ref, q_ref, knew_ref, vnew_ref, bias_ref, gain_ref, *refs):
    del pt_ref
    k_refs, v_refs = refs[:n_pages], refs[n_pages:2 * n_pages]
    o_ref, s_ref = refs[2 * n_pages:]
    heads, hrows, width = q_ref.shape[1:]
    qrows = hrows // 2
    page = k_refs[0].shape[2] // heads
    blocks = n_pages + 1
    lam = sc_ref[0]
    post_scale = sc_ref[1]
    key_block = lambda p: slice(p * page, (p + 1) * page)
    head_rows = lambda h: slice(h * hrows, (h + 1) * hrows)
    head_lanes = lambda h: slice(h * width, (h + 1) * width)
    for h in range(heads):
        qh = q_ref[0, h].astype(BF16)
        for p in range(blocks):
            s_ref[head_rows(h), key_block(p)] = (
                _dot_nt(qh, _head_page(k_refs, knew_ref, p, h, heads, page, head_lanes(h)))
                + bias_ref[head_rows(h), key_block(p)])
    rows = heads * hrows
    new_mask = (lax.broadcasted_iota(jnp.int32, (rows, page), 1)
                <= lax.broadcasted_iota(jnp.int32, (rows, page), 0) % qrows)
    s_ref[:, key_block(n_pages)] = jnp.where(new_mask, s_ref[:, key_block(n_pages)], NEG_INF)
    s = s_ref[...]
    e = jnp.exp(s - jnp.max(s, axis=-1, keepdims=True))
    second_map = lax.broadcasted_iota(jnp.int32, (rows, 1), 0) % hrows >= qrows
    coef = jnp.where(second_map, -lam, 1.0) / jnp.sum(e, axis=-1, keepdims=True)
    s_ref[...] = e * coef
    gain = gain_ref[...]
    for h in range(heads):
        acc = jnp.zeros((hrows, width), F32)
        for p in range(blocks):
            acc = acc + _dot(s_ref[head_rows(h), key_block(p)].astype(BF16),
                             _head_page(v_refs, vnew_ref, p, h, heads, page, head_lanes(h)))
        o = acc[:qrows] + acc[qrows:]
        y = o * lax.rsqrt(jnp.mean(o * o, axis=-1, keepdims=True) + RMS_EPS) * gain
        o_ref[0, :, head_lanes(h)] = y * post_scale


def _dec_call(kind, layer, page_table_flat, n_pages, q, knew, vnew, cache_k, cache_v, extras=()):
    n_samples = q.shape[0]
    page_block = (1, 1) + cache_k.shape[2:]

    def page_spec(p):
        return pl.BlockSpec(page_block, lambda b, pt: (layer, pt[b * n_pages + p], 0, 0))

    per_sample = lambda arr: pl.BlockSpec((1,) + arr.shape[1:], lambda b, pt: (b,) + (0,) * (arr.ndim - 1))
    whole = lambda arr: pl.BlockSpec(arr.shape, lambda b, pt: (0,) * arr.ndim)
    pages = [page_spec(p) for p in range(n_pages)]
    common = [per_sample(q), per_sample(knew), per_sample(vnew)]
    if kind == "sb":
        body = functools.partial(_sb_dec_kernel, n_pages)
        in_specs, operands = common, [q, knew, vnew]
        scratch = [pltpu.VMEM((q.shape[1], (n_pages + 1) * cache_k.shape[3]), F32)]
    else:
        scalars, bias, gain = extras
        body = functools.partial(_df_dec_kernel, n_pages)
        in_specs = [pl.BlockSpec(memory_space=pltpu.SMEM)] + common + [whole(bias), whole(gain)]
        operands = [scalars, q, knew, vnew, bias, gain]
        scratch = [pltpu.VMEM(bias.shape, F32)]
    return pl.pallas_call(
        body,
        grid_spec=pltpu.PrefetchScalarGridSpec(
            num_scalar_prefetch=1,
            grid=(n_samples,),
            in_specs=in_specs + pages + pages,
            out_specs=pl.BlockSpec((1, SUBLANES, GROUP_W), lambda b, pt: (b, 0, 0)),
            scratch_shapes=scratch),
        out_shape=jax.ShapeDtypeStruct((n_samples, SUBLANES, GROUP_W), F32),
        compiler_params=_params(1),
        name=kind + "_dec",
    )(page_table_flat, *operands, *([cache_k] * n_pages), *([cache_v] * n_pages))


def _rel_bucket(dist):
    n = jnp.maximum(dist, 0)
    nf = jnp.maximum(n, 1).astype(F32)
    large = MAX_EXACT + (jnp.log(nf / MAX_EXACT) / math.log(MAX_DISTANCE / MAX_EXACT)
                         * (NUM_BUCKETS - MAX_EXACT)).astype(jnp.int32)
    large = jnp.minimum(large, NUM_BUCKETS - 1)
    return jnp.where(n < MAX_EXACT, n, large)


def _bias_of_dist(rel_bias, dist):
    return rel_bias.astype(F32)[_rel_bucket(dist)].T


def _toeplitz(v, n):
    heads, period = v.shape
    return jnp.tile(v, (1, n))[:, :n * (period - 1)].reshape(heads, n, period - 1)[:, :, :n]


def _prompt_bias_tiles(rel_bias, tq):
    m = jnp.arange(2 * tq, dtype=jnp.int32)
    diag = jnp.where(m < tq, -m, 2 * tq - m)
    prev = jnp.where(m < tq, tq - m, 3 * tq - m)
    return jnp.stack([_toeplitz(_bias_of_dist(rel_bias, diag), tq),
                      _toeplitz(_bias_of_dist(rel_bias, prev), tq)], axis=1)


def _dec_bias(rel_bias, past_len, n_keys, qrows):
    u = _bias_of_dist(rel_bias, past_len + (qrows - 1) - jnp.arange(n_keys + qrows - 1, dtype=jnp.int32))
    per_q = jnp.stack([u[:, qrows - 1 - q:qrows - 1 - q + n_keys] for q in range(qrows)], axis=1)
    heads = per_q.shape[0]
    return jnp.broadcast_to(per_q[:, None], (heads, 2, qrows, n_keys)).reshape(heads * 2 * qrows, n_keys)


def _pad_axis(x, axis, size):
    pad = [(0, 0)] * x.ndim
    pad[axis] = (0, size - x.shape[axis])
    return jnp.pad(x, pad)


def _sb_dec_queries(q):
    n, t, w = q.shape
    qh = _pad_axis(q.astype(F32).reshape(n, 1, t, SB_HEADS, HEAD_DIM), 2, SUBLANES)
    keep = jnp.eye(SB_HEADS, dtype=bool)[None, :, None, :, None]
    return jnp.where(keep, qh, 0.0).reshape(n, SB_HEADS * SUBLANES, w)


def _sb_dec_new_rows(rows, page):
    return _pad_axis(rows.transpose(0, 2, 1), 2, page)


def _df_dec_queries(q):
    n, t, _ = q.shape
    qh = q.astype(F32).reshape(n, t, DF_HEADS, 2, HEAD_DIM).transpose(0, 2, 3, 1, 4)
    qh = _pad_axis(qh, 3, SUBLANES)[:, :, :, :, None, :]
    keep = jnp.eye(2, dtype=bool)[None, None, :, None, :, None]
    return jnp.where(keep, qh, 0.0).reshape(n, DF_HEADS, 2 * SUBLANES, 2 * HEAD_DIM)


def kernel(x_prompt, x_sample, cache_sb_k, cache_sb_v, cache_df_k, cache_df_v, page_table, norm_gains, w_in, w_out, ffa_gate, ffa_up, ffa_down, ffb_gate, ffb_up, ffb_down, df_subln, lambda_q1, lambda_k1, lambda_q2, lambda_k2, rel_bias):
    batch, seq, d_model = x_prompt.shape
    dec_batch, dec_seq, _ = x_sample.shape
    depth, n_phys, page = cache_sb_k.shape[:3]
    n_pages = page_table.shape[1]
    past_len = n_pages * page
    n_prompt, n_sample = batch * seq, dec_batch * dec_seq
    tq = min(ATTN_TILE, seq)
    assert dec_seq <= SUBLANES and seq % tq == 0 and tq > MAX_DISTANCE

    x = jnp.concatenate([x_prompt.reshape(n_prompt, d_model), x_sample.reshape(n_sample, d_model)])
    caches = [jnp.transpose(c, (0, 1, 3, 4, 2)).reshape(depth, n_phys, GROUP_W, page)
              for c in (cache_sb_k, cache_sb_v)]
    caches += [c.reshape(depth, n_phys, page * DF_HEADS, 2 * HEAD_DIM) for c in (cache_df_k, cache_df_v)]
    pt_flat = page_table.reshape(-1).astype(jnp.int32)
    to16 = lambda w: w.astype(BF16)
    w_in16, w_out16 = to16(w_in), to16(w_out)
    ffa = [to16(w) for w in (ffa_gate, ffa_up, ffa_down)]
    ffb = [to16(w) for w in (ffb_gate, ffb_up, ffb_down)]
    gains = norm_gains.astype(F32)

    bias_tiles = _prompt_bias_tiles(rel_bias, tq)
    bias_dec = _dec_bias(rel_bias, past_len, past_len + page, SUBLANES)
    far_bias = rel_bias.astype(F32)[NUM_BUCKETS - 1]

    carried = None
    sb_prompt_rows = ([], [])
    for l in range(depth):
        lam_init = 0.8 - 0.6 * math.exp(-0.3 * l)
        lam = (jnp.exp(jnp.sum(lambda_q1[l].astype(F32) * lambda_k1[l].astype(F32)))
               - jnp.exp(jnp.sum(lambda_q2[l].astype(F32) * lambda_k2[l].astype(F32)))
               + lam_init)
        scalars = jnp.concatenate([jnp.stack([lam, jnp.asarray(1.0 - lam_init, F32)]), far_bias]).astype(F32)
        gain = df_subln[l].astype(F32).reshape(1, 2 * HEAD_DIM)

        x1, qkv, new_rows, carried = _pre_call(l, x, n_prompt, gains, *ffa, w_in16, carried)
        sb_prompt_rows[0].append(new_rows[0])
        sb_prompt_rows[1].append(new_rows[2])
        ksb_s, vsb_s, kdf_s, vdf_s = [a[l].reshape(dec_batch, dec_seq, GROUP_W) for a in new_rows[1::2]]

        osb_p = _sb_prompt_call(qkv, batch, seq, tq)
        odf_p = _df_prompt_call(qkv, scalars, bias_tiles, gain, batch, seq, tq)

        qkv_s = qkv[n_prompt:].reshape(dec_batch, dec_seq, N_GROUPS * GROUP_W)
        new = lambda rows: _pad_axis(rows, 1, SUBLANES)
        osb_s = _dec_call("sb", l, pt_flat, n_pages, _sb_dec_queries(qkv_s[..., :GROUP_W]),
                          _sb_dec_new_rows(ksb_s, page), _sb_dec_new_rows(vsb_s, page), caches[0], caches[1])
        odf_s = _dec_call("df", l, pt_flat, n_pages, _df_dec_queries(qkv_s[..., 3 * GROUP_W:4 * GROUP_W]),
                          new(kdf_s), new(vdf_s), caches[2], caches[3], (scalars, bias_dec, gain))

        tail = lambda o: o[:, :dec_seq].reshape(n_sample, GROUP_W).astype(BF16)
        x = _post_call(l, x1, n_prompt, osb_p, tail(osb_s), odf_p, tail(odf_s), gains, w_out16, *ffb)

    y_prompt = x[:n_prompt].reshape(batch, seq, d_model)
    y_sample = x[n_prompt:].reshape(dec_batch, dec_seq, d_model)
    shapes = ((SB_HEADS, HEAD_DIM), (SB_HEADS, HEAD_DIM), (DF_HEADS, 2 * HEAD_DIM), (DF_HEADS, 2 * HEAD_DIM))
    prompt_rows = [jnp.stack([a.reshape(batch, seq, *shapes[0]) for a in rows]) for rows in sb_prompt_rows]
    prompt_rows += [a.reshape(depth, batch, seq, *s) for a, s in zip(new_rows[4::2], shapes[2:])]
    sample_rows = [a.reshape(depth, dec_batch, dec_seq, *s) for a, s in zip(new_rows[1::2], shapes)]
    return (y_prompt, y_sample, *prompt_rows, *sample_rows)
```

```python
import functools
import math

import jax
import jax.numpy as jnp
from jax import lax
from jax.experimental import pallas as pl
from jax.experimental.pallas import tpu as pltpu

F32 = jnp.float32
BF16 = jnp.bfloat16

HEAD_DIM = 64
SB_HEADS = 8
DF_HEADS = 4
GROUP_W = SB_HEADS * HEAD_DIM
N_GROUPS = 6
NUM_BUCKETS = 32
MAX_EXACT = NUM_BUCKETS // 2
MAX_DISTANCE = 128
RMS_EPS = 1e-6
NEG_INF = -1e30
QK_SCALE = HEAD_DIM ** -0.5
LOG2E = 1.4426950408889634

LANES = 128
SUBLANES = 8
TOKEN_TILE = 512
ATTN_TILE = 256
ATTN_BLOCK_W = 256
SB_PHASE_GROUP = 2
FF_CHUNK = 256
VMEM_LIMIT_BYTES = 56 * 1024 * 1024


def _dot(a, b):
    return jnp.dot(a, b, preferred_element_type=F32)


def _dot_nt(a, b):
    return lax.dot_general(a, b, (((1,), (1,)), ((), ())), preferred_element_type=F32)


def _rms(x, g):
    ms = jnp.mean(x * x, axis=-1, keepdims=True)
    return x * lax.rsqrt(ms + RMS_EPS) * g


def _log2_sigmoid(z2):
    neg_abs = pltpu.bitcast(pltpu.bitcast(z2, jnp.uint32) | jnp.uint32(0x80000000), F32)
    return jnp.minimum(z2, 0.0) - jnp.log2(1.0 + jnp.exp2(neg_abs))


def _later_matrix(n):
    r = lax.broadcasted_iota(jnp.int32, (n, n), 0)
    c = lax.broadcasted_iota(jnp.int32, (n, n), 1)
    return (r > c).astype(BF16)


def _swiglu_residual(x, g_pre, g_post, wg_ref, wu_ref, wd_ref):
    xn = _rms(x, g_pre).astype(BF16)
    d_ff = wg_ref.shape[1]
    acc = None
    for c0 in range(0, d_ff, FF_CHUNK):
        c1 = min(c0 + FF_CHUNK, d_ff)
        gate = _dot(xn, wg_ref[:, c0:c1])
        up = _dot(xn, wu_ref[:, c0:c1])
        h = (gate * jax.nn.sigmoid(gate) * up).astype(BF16)
        y = _dot(h, wd_ref[c0:c1, :])
        acc = y if acc is None else acc + y
    return x + 0.5 * _rms(acc, g_post)


def _pre_kernel(prompt_tiles, n_carried, x_ref, g_ref, wg_ref, wu_ref, wd_ref, win_ref, *refs):
    x1_ref, qkv_ref = refs[n_carried:n_carried + 2]
    kv_refs = refs[n_carried + 2:]
    is_prompt = pl.program_id(0) < prompt_tiles
    g = g_ref[...]
    x1 = _swiglu_residual(x_ref[...], g[0:1], g[1:2], wg_ref, wu_ref, wd_ref)
    x1_ref[...] = x1
    h = _rms(x1, g[2:3]).astype(BF16)
    kv_pair = {1: 0, 2: 1, 4: 2, 5: 3}
    for c in range(N_GROUPS):
        lo, hi = c * GROUP_W, (c + 1) * GROUP_W
        p = _dot(h, win_ref[:, lo:hi])
        if c in kv_pair:
            prompt_ref, sample_ref = kv_refs[2 * kv_pair[c]:2 * kv_pair[c] + 2]

            sample_ref[...] = p

            @pl.when(is_prompt)
            def _(p=p, ref=prompt_ref):
                ref[...] = p
        else:
            p = p * QK_SCALE
        qkv_ref[:, lo:hi] = p.astype(BF16)


def _post_kernel(prompt_tiles, x_ref, osb_p_ref, osb_s_ref, odf_p_ref, odf_s_ref,
                 g_ref, wout_ref, wg_ref, wu_ref, wd_ref, y_ref):
    is_prompt = pl.program_id(0) < prompt_tiles
    osb = jnp.where(is_prompt, osb_p_ref[...], osb_s_ref[...])
    odf = jnp.where(is_prompt, odf_p_ref[...], odf_s_ref[...])
    g = g_ref[...]
    mix = _dot(osb, wout_ref[0:GROUP_W, :]) + _dot(odf, wout_ref[GROUP_W:, :])
    x2 = x_ref[...] + _rms(mix, g[3:4])
    y_ref[...] = _swiglu_residual(x2, g[4:5], g[5:6], wg_ref, wu_ref, wd_ref)


def _token_tile(n):
    t = TOKEN_TILE
    while n % t:
        t //= 2
    return t


def _resident(stacked, layer):
    return pl.BlockSpec((None,) + stacked.shape[1:], lambda i: (layer, 0, 0), pipeline_mode=pl.Buffered(1))


def _rows(tm, width):
    return pl.BlockSpec((tm, width), lambda i: (i, 0))


def _params(n_axes):
    return pltpu.CompilerParams(dimension_semantics=("parallel",) * n_axes, vmem_limit_bytes=VMEM_LIMIT_BYTES)


def _prompt_rows(tm, width, prompt_tiles, layer=None):
    index = lambda i: jnp.minimum(i, prompt_tiles - 1)
    if layer is None:
        return pl.BlockSpec((tm, width), lambda i: (index(i), 0))
    return pl.BlockSpec((None, tm, width), lambda i: (layer, index(i), 0))


def _sample_rows(tm, width, prompt_tiles, layer=None):
    index = lambda i: jnp.maximum(i - prompt_tiles, 0)
    if layer is None:
        return pl.BlockSpec((tm, width), lambda i: (index(i), 0))
    return pl.BlockSpec((None, tm, width), lambda i: (layer, index(i), 0))


def _sequential():
    return pltpu.CompilerParams(dimension_semantics=("arbitrary",), vmem_limit_bytes=VMEM_LIMIT_BYTES)


def _pre_call(layer, x, n_prompt, gains, wg, wu, wd, win, carried):
    n, d = x.shape
    depth = gains.shape[0]
    tm = _token_tile(math.gcd(n_prompt, n - n_prompt))
    prompt_tiles = n_prompt // tm
    carried = list(carried or [])
    kv_specs, kv_shapes, stacked_at = [], [], []
    for kind in range(4):
        if kind < 2:
            kv_specs.append(_prompt_rows(tm, GROUP_W, prompt_tiles))
            kv_shapes.append(jax.ShapeDtypeStruct((n_prompt, GROUP_W), F32))
        else:
            stacked_at.append(len(kv_specs))
            kv_specs.append(_prompt_rows(tm, GROUP_W, prompt_tiles, layer))
            kv_shapes.append(jax.ShapeDtypeStruct((depth, n_prompt, GROUP_W), F32))
        stacked_at.append(len(kv_specs))
        kv_specs.append(_sample_rows(tm, GROUP_W, prompt_tiles, layer))
        kv_shapes.append(jax.ShapeDtypeStruct((depth, n - n_prompt, GROUP_W), F32))
    n_fixed_in, n_fixed_out = 6, 2
    outs = pl.pallas_call(
        functools.partial(_pre_kernel, prompt_tiles, len(carried)),
        grid=(n // tm,),
        in_specs=([_rows(tm, d)] + [_resident(w, layer) for w in (gains, wg, wu, wd, win)]
                  + [pl.BlockSpec(memory_space=pl.ANY)] * len(carried)),
        out_specs=[_rows(tm, d), _rows(tm, N_GROUPS * GROUP_W)] + kv_specs,
        out_shape=[jax.ShapeDtypeStruct((n, d), F32),
                   jax.ShapeDtypeStruct((n, N_GROUPS * GROUP_W), BF16)] + kv_shapes,
        input_output_aliases={n_fixed_in + k: n_fixed_out + stacked_at[k] for k in range(len(carried))},
        compiler_params=_sequential(),
        name="pre",
    )(x, gains, wg, wu, wd, win, *carried)
    kv = outs[n_fixed_out:]
    return outs[0], outs[1], kv, [kv[j] for j in stacked_at]


def _post_call(layer, x, n_prompt, osb_p, osb_s, odf_p, odf_s, gains, wout, wg, wu, wd):
    n, d = x.shape
    tm = _token_tile(math.gcd(n_prompt, n - n_prompt))
    prompt_tiles = n_prompt // tm
    attn = [_prompt_rows(tm, GROUP_W, prompt_tiles), _sample_rows(tm, GROUP_W, prompt_tiles)]
    return pl.pallas_call(
        functools.partial(_post_kernel, prompt_tiles),
        grid=(n // tm,),
        in_specs=[_rows(tm, d)] + attn + attn + [_resident(w, layer) for w in (gains, wout, wg, wu, wd)],
        out_specs=_rows(tm, d),
        out_shape=jax.ShapeDtypeStruct((n, d), F32),
        compiler_params=_sequential(),
        name="post",
    )(x, osb_p, osb_s, odf_p, odf_s, gains, wout, wg, wu, wd)


def _sb_prompt_kernel(q_ref, k_ref, v_ref, o_ref):
    i = pl.program_id(2)
    tq, width = q_ref.shape
    heads = width // HEAD_DIM
    later = _later_matrix(tq)
    r = lax.broadcasted_iota(jnp.int32, (tq, tq), 0)
    c = lax.broadcasted_iota(jnp.int32, (tq, tq), 1)
    diag_mask = c < r
    lanes = [slice(h * HEAD_DIM, (h + 1) * HEAD_DIM) for h in range(heads)]
    qs = [q_ref[:, sl] for sl in lanes]
    groups = [list(range(g, g + SB_PHASE_GROUP)) for g in range(0, heads, SB_PHASE_GROUP)]

    def scores(rows, hs):
        return [_dot_nt(qs[h], k_ref[rows, lanes[h]]) for h in hs]

    def log_terms(zs, mask):
        out = []
        for z in zs:
            z2 = z * LOG2E
            ls = _log2_sigmoid(z2)
            lstay = ls - z2
            if mask is not None:
                lstay = jnp.where(mask, lstay, 0.0)
            hi = lstay.astype(BF16)
            out.append((ls, lstay, hi, (lstay - hi.astype(F32)).astype(BF16)))
        return out

    def cumsums(terms):
        n = len(terms)
        local = _dot(jnp.concatenate([t[2] for t in terms] + [t[3] for t in terms], axis=0), later)
        return [local[k * tq:(k + 1) * tq] + local[(n + k) * tq:(n + k + 1) * tq] for k in range(n)]

    def weights(terms, local, runs, mask):
        ws, new_runs = [], []
        for (ls, lstay, _, _), loc, run in zip(terms, local, runs):
            w = jnp.exp2(ls + loc + run)
            if mask is not None:
                w = jnp.where(mask, w, 0.0)
            ws.append(w.astype(BF16))
            new_runs.append(run + jnp.sum(lstay, axis=-1, keepdims=True))
        return ws, new_runs

    def values(rows, hs, ws, accs):
        return [acc + _dot(w, v_ref[rows, lanes[h]]) for h, w, acc in zip(hs, ws, accs)]

    def tiles(js, state, mask):
        units = [(pl.ds(pl.multiple_of(j * tq, tq), tq), g) for j in js for g in range(len(groups))]
        n = len(units)
        accs = [[state[h][0] for h in hs] for hs in groups]
        runs = [[state[h][1] for h in hs] for hs in groups]
        z, terms, local, ws = [None] * n, [None] * n, [None] * n, [None] * n
        for step in range(n + 4):
            for u, (rows, g) in enumerate(units):
                phase = step - u
                if phase == 0:
                    z[u] = scores(rows, groups[g])
                elif phase == 1:
                    terms[u] = log_terms(z[u], mask)
                elif phase == 2:
                    local[u] = cumsums(terms[u])
                elif phase == 3:
                    ws[u], runs[g] = weights(terms[u], local[u], runs[g], mask)
                elif phase == 4:
                    accs[g] = values(rows, groups[g], ws[u], accs[g])
        out = [None] * heads
        for g, hs in enumerate(groups):
            for k, h in enumerate(hs):
                out[h] = (accs[g][k], runs[g][k])
        return tuple(out)

    state = tuple((jnp.zeros((tq, HEAD_DIM), F32), jnp.zeros((tq, 1), F32)) for _ in range(heads))
    state = tiles([i], state, diag_mask)
    pairs = i // 2
    state = lax.fori_loop(0, pairs, lambda t, s: tiles([i - 1 - 2 * t, i - 2 - 2 * t], s, None), state)
    state = lax.fori_loop(0, i - 2 * pairs, lambda t, s: tiles([0], s, None), state)
    for h in range(heads):
        o_ref[:, lanes[h]] = state[h][0].astype(o_ref.dtype)


def _df_prompt_kernel(sc_ref, q_ref, k_ref, v_ref, bias_ref, gain_ref, o_ref):
    block = pl.program_id(1)
    i = pl.program_id(2)
    tq, width = q_ref.shape
    heads = width // LANES
    lam = sc_ref[0]
    post_scale = sc_ref[1]
    r = lax.broadcasted_iota(jnp.int32, (tq, tq), 0)
    c = lax.broadcasted_iota(jnp.int32, (tq, tq), 1)
    maps = [(h, m) for h in range(heads) for m in range(2)]
    lane = lambda h, m: slice(h * LANES + m * HEAD_DIM, h * LANES + (m + 1) * HEAD_DIM)
    qs = {hm: q_ref[:, lane(*hm)] for hm in maps}

    def scores(rows, h, bias_of, mask):
        out = []
        for m in range(2):
            s = _dot_nt(qs[(h, m)], k_ref[rows, lane(h, m)]) + bias_of(h)
            out.append(s if mask is None else jnp.where(mask, s, NEG_INF))
        return out

    def softmax_terms(ss, old):
        out = []
        for s, (mx, l, _) in zip(ss, old):
            m_new = jnp.maximum(mx, jnp.max(s, axis=-1, keepdims=True))
            alpha = jnp.exp(mx - m_new)
            p = jnp.exp(s - m_new)
            out.append((m_new, alpha * l + jnp.sum(p, axis=-1, keepdims=True), alpha, p.astype(BF16)))
        return out

    def values(rows, h, terms, old):
        vt = v_ref[rows, h * LANES:(h + 1) * LANES]
        return [(m_new, l, alpha * acc + _dot(p, vt)) for (m_new, l, alpha, p), (_, _, acc) in zip(terms, old)]

    def tile(j, state, bias_of, mask):
        rows = pl.ds(pl.multiple_of(j * tq, tq), tq)
        old = [[state[(h, m)] for m in range(2)] for h in range(heads)]
        ss, terms, new = [None] * heads, [None] * heads, [None] * heads
        for step in range(heads + 2):
            for h in range(heads):
                phase = step - h
                if phase == 0:
                    ss[h] = scores(rows, h, bias_of, mask)
                elif phase == 1:
                    terms[h] = softmax_terms(ss[h], old[h])
                elif phase == 2:
                    new[h] = values(rows, h, terms[h], old[h])
        return {(h, m): new[h][m] for h, m in maps}

    init = (jnp.full((tq, 1), NEG_INF, F32), jnp.zeros((tq, 1), F32), jnp.zeros((tq, LANES), F32))
    state = tile(i, {hm: init for hm in maps}, lambda h: bias_ref[h, 0], c <= r)
    has_prev = jnp.broadcast_to(i >= 1, (tq, tq))
    state = tile(jnp.maximum(i - 1, 0), state, lambda h: bias_ref[h, 1], has_prev)
    state = lax.fori_loop(0, jnp.maximum(i - 1, 0),
                          lambda j, s: tile(j, s, lambda h: sc_ref[2 + block * heads + h], None), state)
    for h in range(heads):
        (_, l1, a1), (_, l2, a2) = state[(h, 0)], state[(h, 1)]
        o = a1 / l1 - lam * (a2 / l2)
        y = o * lax.rsqrt(jnp.mean(o * o, axis=-1, keepdims=True) + RMS_EPS) * gain_ref[...]
        o_ref[:, h * LANES:(h + 1) * LANES] = (y * post_scale).astype(o_ref.dtype)


def _prompt_attention_call(body, name, qkv, group, batch, seq, tq, extra_operands=(), extra_specs=()):
    nq = seq // tq
    blocks = GROUP_W // ATTN_BLOCK_W
    col = lambda g: (lambda b, p, i: (b, g * blocks + p))
    q_spec = pl.BlockSpec((tq, ATTN_BLOCK_W), lambda b, p, i: (b * nq + i, group * blocks + p))
    kv_spec = lambda g: pl.BlockSpec((seq, ATTN_BLOCK_W), col(g))
    return pl.pallas_call(
        body,
        grid=(batch, blocks, nq),
        in_specs=list(extra_specs[:1]) + [q_spec, kv_spec(group + 1), kv_spec(group + 2)] + list(extra_specs[1:]),
        out_specs=pl.BlockSpec((tq, ATTN_BLOCK_W), lambda b, p, i: (b * nq + i, p)),
        out_shape=jax.ShapeDtypeStruct((batch * seq, GROUP_W), BF16),
        compiler_params=_params(3),
        name=name,
    )(*extra_operands[:1], qkv, qkv, qkv, *extra_operands[1:])


def _sb_prompt_call(qkv, batch, seq, tq):
    return _prompt_attention_call(_sb_prompt_kernel, "sb_prompt", qkv, 0, batch, seq, tq)


def _df_prompt_call(qkv, scalars, bias_tiles, gain, batch, seq, tq):
    heads = ATTN_BLOCK_W // LANES
    specs = [pl.BlockSpec(memory_space=pltpu.SMEM),
             pl.BlockSpec((heads, 2, tq, tq), lambda b, p, i: (p, 0, 0, 0)),
             pl.BlockSpec((1, LANES), lambda b, p, i: (0, 0))]
    return _prompt_attention_call(_df_prompt_kernel, "df_prompt", qkv, 3, batch, seq, tq,
                                  (scalars, bias_tiles, gain), specs)


def _new_page(new_ref, lanes, page):
    rows = new_ref[0][:, lanes]
    pad = jnp.zeros((page - rows.shape[0], rows.shape[1]), F32)
    return jnp.concatenate([rows, pad], axis=0).astype(BF16)


def _head_page(refs, new_ref, p, h, heads, page, lanes):
    if p == len(refs):
        return _new_page(new_ref, lanes, page)
    return refs[p][0, 0, pl.ds(h, page, stride=heads), :].astype(BF16)


def _sb_dec_kernel(n_pages, pt_ref, q_ref, knew_ref, vnew_ref, *refs):
    del pt_ref
    k_refs, v_refs = refs[:n_pages], refs[n_pages:2 * n_pages]
    o_ref, s_ref = refs[2 * n_pages:]
    rows, width = q_ref.shape[1:]
    page = k_refs[0].shape[3]
    blocks = n_pages + 1
    qrows = rows // SB_HEADS
    key_block = lambda p: slice(p * page, (p + 1) * page)
    page_of = lambda pages, new_ref, p: (new_ref[0] if p == n_pages else pages[p][0, 0]).astype(BF16)
    q = q_ref[0].astype(BF16)
    for p in range(blocks):
        s_ref[:, key_block(p)] = _dot(q, page_of(k_refs, knew_ref, p))
    z = s_ref[...]
    lane = lax.broadcasted_iota(jnp.int32, z.shape, 1)
    query = lax.broadcasted_iota(jnp.int32, z.shape, 0) % qrows
    mask = lane < n_pages * page + query
    z2 = z * LOG2E
    ls = _log2_sigmoid(z2)
    lstay = jnp.where(mask, ls - z2, 0.0)
    hi = lstay.astype(BF16)
    lo = (lstay - hi.astype(F32)).astype(BF16)
    stacked = jnp.concatenate([part[:, key_block(p)] for part in (hi, lo) for p in range(blocks)], axis=0)
    local = _dot(stacked, _later_matrix(page))
    run = jnp.zeros((rows, 1), F32)
    acc = jnp.zeros((rows, width), F32)
    for p in range(blocks - 1, -1, -1):
        after = local[p * rows:(p + 1) * rows] + local[(blocks + p) * rows:(blocks + p + 1) * rows] + run
        w = jnp.where(mask[:, key_block(p)], jnp.exp2(ls[:, key_block(p)] + after), 0.0)
        acc = acc + _dot_nt(w.astype(BF16), page_of(v_refs, vnew_ref, p))
        run = run + jnp.sum(lstay[:, key_block(p)], axis=-1, keepdims=True)
    for h in range(SB_HEADS):
        lanes = slice(h * HEAD_DIM, (h + 1) * HEAD_DIM)
        o_ref[0, :, lanes] = acc[h * qrows:(h + 1) * qrows, lanes]


def _df_dec_kernel(n_pages, pt_ref, sc_ref, q_ref, knew_ref, vnew_ref, bias_ref, gain_ref, *refs):
    del pt_ref
    k_refs, v_refs = refs[:n_pages], refs[n_pages:2 * n_pages]
    o_ref, s_ref = refs[2 * n_pages:]
    heads, hrows, width = q_ref.shape[1:]
    qrows = hrows // 2
    page = k_refs[0].shape[2] // heads
    blocks = n_pages + 1
    lam = sc_ref[0]
    post_scale = sc_ref[1]
    key_block = lambda p: slice(p * page, (p + 1) * page)
    head_rows = lambda h: slice(h * hrows, (h + 1) * hrows)
    head_lanes = lambda h: slice(h * width, (h + 1) * width)
    for h in range(heads):
        qh = q_ref[0, h].astype(BF16)
        for p in range(blocks):
            s_ref[head_rows(h), key_block(p)] = (
                _dot_nt(qh, _head_page(k_refs, knew_ref, p, h, heads, page, head_lanes(h)))
                + bias_ref[head_rows(h), key_block(p)])
    rows = heads * hrows
    new_mask = (lax.broadcasted_iota(jnp.int32, (rows, page), 1)
                <= lax.broadcasted_iota(jnp.int32, (rows, page), 0) % qrows)
    s_ref[:, key_block(n_pages)] = jnp.where(new_mask, s_ref[:, key_block(n_pages)], NEG_INF)
    s = s_ref[...]
    e = jnp.exp(s - jnp.max(s, axis=-1, keepdims=True))
    second_map = lax.broadcasted_iota(jnp.int32, (rows, 1), 0) % hrows >= qrows
    coef = jnp.where(second_map, -lam, 1.0) / jnp.sum(e, axis=-1, keepdims=True)
    s_ref[...] = e * coef
    gain = gain_ref[...]
    for h in range(heads):
        acc = jnp.zeros((hrows, width), F32)
        for p in range(blocks):
            acc = acc + _dot(s_ref[head_rows(h), key_block(p)].astype(BF16),
                             _head_page(v_refs, vnew_ref, p, h, heads, page, head_lanes(h)))
        o = acc[:qrows] + acc[qrows:]
        y = o * lax.rsqrt(jnp.mean(o * o, axis=-1, keepdims=True) + RMS_EPS) * gain
        o_ref[0, :, head_lanes(h)] = y * post_scale


def _dec_call(kind, layer, page_table_flat, n_pages, q, knew, vnew, cache_k, cache_v, extras=()):
    n_samples = q.shape[0]
    page_block = (1, 1) + cache_k.shape[2:]

    def page_spec(p):
        return pl.BlockSpec(page_block, lambda b, pt: (layer, pt[b * n_pages + p], 0, 0))

    per_sample = lambda arr: pl.BlockSpec((1,) + arr.shape[1:], lambda b, pt: (b,) + (0,) * (arr.ndim - 1))
    whole = lambda arr: pl.BlockSpec(arr.shape, lambda b, pt: (0,) * arr.ndim)
    pages = [page_spec(p) for p in range(n_pages)]
    common = [per_sample(q), per_sample(knew), per_sample(vnew)]
    if kind == "sb":
        body = functools.partial(_sb_dec_kernel, n_pages)
        in_specs, operands = common, [q, knew, vnew]
        scratch = [pltpu.VMEM((q.shape[1], (n_pages + 1) * cache_k.shape[3]), F32)]
    else:
        scalars, bias, gain = extras
        body = functools.partial(_df_dec_kernel, n_pages)
        in_specs = [pl.BlockSpec(memory_space=pltpu.SMEM)] + common + [whole(bias), whole(gain)]
        operands = [scalars, q, knew, vnew, bias, gain]
        scratch = [pltpu.VMEM(bias.shape, F32)]
    return pl.pallas_call(
        body,
        grid_spec=pltpu.PrefetchScalarGridSpec(
            num_scalar_prefetch=1,
            grid=(n_samples,),
            in_specs=in_specs + pages + pages,
            out_specs=pl.BlockSpec((1, SUBLANES, GROUP_W), lambda b, pt: (b, 0, 0)),
            scratch_shapes=scratch),
        out_shape=jax.ShapeDtypeStruct((n_samples, SUBLANES, GROUP_W), F32),
        compiler_params=_params(1),
        name=kind + "_dec",
    )(page_table_flat, *operands, *([cache_k] * n_pages), *([cache_v] * n_pages))


def _rel_bucket(dist):
    n = jnp.maximum(dist, 0)
    nf = jnp.maximum(n, 1).astype(F32)
    large = MAX_EXACT + (jnp.log(nf / MAX_EXACT) / math.log(MAX_DISTANCE / MAX_EXACT)
                         * (NUM_BUCKETS - MAX_EXACT)).astype(jnp.int32)
    large = jnp.minimum(large, NUM_BUCKETS - 1)
    return jnp.where(n < MAX_EXACT, n, large)


def _bias_of_dist(rel_bias, dist):
    return rel_bias.astype(F32)[_rel_bucket(dist)].T


def _toeplitz(v, n):
    heads, period = v.shape
    return jnp.tile(v, (1, n))[:, :n * (period - 1)].reshape(heads, n, period - 1)[:, :, :n]


def _prompt_bias_tiles(rel_bias, tq):
    m = jnp.arange(2 * tq, dtype=jnp.int32)
    diag = jnp.where(m < tq, -m, 2 * tq - m)
    prev = jnp.where(m < tq, tq - m, 3 * tq - m)
    return jnp.stack([_toeplitz(_bias_of_dist(rel_bias, diag), tq),
                      _toeplitz(_bias_of_dist(rel_bias, prev), tq)], axis=1)


def _dec_bias(rel_bias, past_len, n_keys, qrows):
    u = _bias_of_dist(rel_bias, past_len + (qrows - 1) - jnp.arange(n_keys + qrows - 1, dtype=jnp.int32))
    per_q = jnp.stack([u[:, qrows - 1 - q:qrows - 1 - q + n_keys] for q in range(qrows)], axis=1)
    heads = per_q.shape[0]
    return jnp.broadcast_to(per_q[:, None], (heads, 2, qrows, n_keys)).reshape(heads * 2 * qrows, n_keys)


def _pad_axis(x, axis, size):
    pad = [(0, 0)] * x.ndim
    pad[axis] = (0, size - x.shape[axis])
    return jnp.pad(x, pad)


def _sb_dec_queries(q):
    n, t, w = q.shape
    qh = _pad_axis(q.astype(F32).reshape(n, 1, t, SB_HEADS, HEAD_DIM), 2, SUBLANES)
    keep = jnp.eye(SB_HEADS, dtype=bool)[None, :, None, :, None]
    return jnp.where(keep, qh, 0.0).reshape(n, SB_HEADS * SUBLANES, w)


def _sb_dec_new_rows(rows, page):
    return _pad_axis(rows.transpose(0, 2, 1), 2, page)


def _df_dec_queries(q):
    n, t, _ = q.shape
    qh = q.astype(F32).reshape(n, t, DF_HEADS, 2, HEAD_DIM).transpose(0, 2, 3, 1, 4)
    qh = _pad_axis(qh, 3, SUBLANES)[:, :, :, :, None, :]
    keep = jnp.eye(2, dtype=bool)[None, None, :, None, :, None]
    return jnp.where(keep, qh, 0.0).reshape(n, DF_HEADS, 2 * SUBLANES, 2 * HEAD_DIM)


def kernel(x_prompt, x_sample, cache_sb_k, cache_sb_v, cache_df_k, cache_df_v, page_table, norm_gains, w_in, w_out, ffa_gate, ffa_up, ffa_down, ffb_gate, ffb_up, ffb_down, df_subln, lambda_q1, lambda_k1, lambda_q2, lambda_k2, rel_bias):
    batch, seq, d_model = x_prompt.shape
    dec_batch, dec_seq, _ = x_sample.shape
    depth, n_phys, page = cache_sb_k.shape[:3]
    n_pages = page_table.shape[1]
    past_len = n_pages * page
    n_prompt, n_sample = batch * seq, dec_batch * dec_seq
    tq = min(ATTN_TILE, seq)
    assert dec_seq <= SUBLANES and seq % tq == 0 and tq > MAX_DISTANCE

    x = jnp.concatenate([x_prompt.reshape(n_prompt, d_model), x_sample.reshape(n_sample, d_model)])
    caches = [jnp.transpose(c, (0, 1, 3, 4, 2)).reshape(depth, n_phys, GROUP_W, page)
              for c in (cache_sb_k, cache_sb_v)]
    caches += [c.reshape(depth, n_phys, page * DF_HEADS, 2 * HEAD_DIM) for c in (cache_df_k, cache_df_v)]
    pt_flat = page_table.reshape(-1).astype(jnp.int32)
    to16 = lambda w: w.astype(BF16)
    w_in16, w_out16 = to16(w_in), to16(w_out)
    ffa = [to16(w) for w in (ffa_gate, ffa_up, ffa_down)]
    ffb = [to16(w) for w in (ffb_gate, ffb_up, ffb_down)]
    gains = norm_gains.astype(F32)

    bias_tiles = _prompt_bias_tiles(rel_bias, tq)
    bias_dec = _dec_bias(rel_bias, past_len, past_len + page, SUBLANES)
    far_bias = rel_bias.astype(F32)[NUM_BUCKETS - 1]

    carried = None
    sb_prompt_rows = ([], [])
    lam_inits = [0.8 - 0.6 * math.exp(-0.3 * l) for l in range(depth)]
    lam = (jnp.exp(jnp.sum(lambda_q1.astype(F32) * lambda_k1.astype(F32), axis=-1))
           - jnp.exp(jnp.sum(lambda_q2.astype(F32) * lambda_k2.astype(F32), axis=-1))
           + jnp.asarray(lam_inits, F32))
    post_scale = jnp.asarray([1.0 - v for v in lam_inits], F32)
    all_scalars = jnp.concatenate([lam[:, None], post_scale[:, None],
                                   jnp.broadcast_to(far_bias, (depth, DF_HEADS))], axis=1)
    subln = df_subln.astype(F32)
    for l in range(depth):
        scalars = all_scalars[l]
        gain = subln[l].reshape(1, 2 * HEAD_DIM)

        x1, qkv, new_rows, carried = _pre_call(l, x, n_prompt, gains, *ffa, w_in16, carried)
        sb_prompt_rows[0].append(new_rows[0])
        sb_prompt_rows[1].append(new_rows[2])
        ksb_s, vsb_s, kdf_s, vdf_s = [a[l].reshape(dec_batch, dec_seq, GROUP_W) for a in new_rows[1::2]]

        osb_p = _sb_prompt_call(qkv, batch, seq, tq)
        odf_p = _df_prompt_call(qkv, scalars, bias_tiles, gain, batch, seq, tq)

        qkv_s = qkv[n_prompt:].reshape(dec_batch, dec_seq, N_GROUPS * GROUP_W)
        new = lambda rows: _pad_axis(rows, 1, SUBLANES)
        osb_s = _dec_call("sb", l, pt_flat, n_pages, _sb_dec_queries(qkv_s[..., :GROUP_W]),
                          _sb_dec_new_rows(ksb_s, page), _sb_dec_new_rows(vsb_s, page), caches[0], caches[1])
        odf_s = _dec_call("df", l, pt_flat, n_pages, _df_dec_queries(qkv_s[..., 3 * GROUP_W:4 * GROUP_W]),
                          new(kdf_s), new(vdf_s), caches[2], caches[3], (scalars, bias_dec, gain))

        tail = lambda o: o[:, :dec_seq].reshape(n_sample, GROUP_W).astype(BF16)
        x = _post_call(l, x1, n_prompt, osb_p, tail(osb_s), odf_p, tail(odf_s), gains, w_out16, *ffb)

    y_prompt = x[:n_prompt].reshape(batch, seq, d_model)
    y_sample = x[n_prompt:].reshape(dec_batch, dec_seq, d_model)
    shapes = ((SB_HEADS, HEAD_DIM), (SB_HEADS, HEAD_DIM), (DF_HEADS, 2 * HEAD_DIM), (DF_HEADS, 2 * HEAD_DIM))
    prompt_rows = [jnp.stack([a.reshape(batch, seq, *shapes[0]) for a in rows]) for rows in sb_prompt_rows]
    prompt_rows += [a.reshape(depth, batch, seq, *s) for a, s in zip(new_rows[4::2], shapes[2:])]
    sample_rows = [a.reshape(depth, dec_batch, dec_seq, *s) for a, s in zip(new_rows[1::2], shapes)]
    return (y_prompt, y_sample, *prompt_rows, *sample_rows)
```
